```python
import jax, jax.numpy as jnp
from jax import lax
import numpy as np

D_MODEL = 1024
BATCH = 4
SEQ = 4096
DEPTH = 2
DEC_BATCH = 128
DEC_SEQ = 8
PAST_LEN = 8192
PAGE_SIZE = 128

HEAD_DIM = 64
MIX_WIDTH = D_MODEL
C_RWKV = MIX_WIDTH // 2
H_RWKV = C_RWKV // HEAD_DIM
C_ATTN = MIX_WIDTH - C_RWKV
H_ATTN = C_ATTN // HEAD_DIM
KV_HEADS = 2
GQA_GROUP = H_ATTN // KV_HEADS
WINDOW = 128
ATTN_BLOCK = 128
D_DECAY = 64
D_AAA = 64
D_MV = 32
D_GATE = 128
N_SHIFT = 3 * C_RWKV + D_DECAY + D_AAA + D_GATE
N_IN = N_SHIFT + C_ATTN + 2 * KV_HEADS * HEAD_DIM
D_FF = 2816
N_EXPERTS = 8
TOP_K = 2
MOE_BLOCK = 128
RMS_EPS = 1e-6
LNX_EPS = 64e-5
NEG = -1e30

kernel_name = 'hybrid_rwkv7_swa_moe_step'


def _rmsnorm(x, g):
    xf = x.astype(jnp.float32)
    y = xf * lax.rsqrt(jnp.mean(xf * xf, axis=-1, keepdims=True) + RMS_EPS)
    return (y * g.astype(jnp.float32)).astype(x.dtype)


def _head_groupnorm(o, w, b):
    mu = jnp.mean(o, axis=-1, keepdims=True)
    var = jnp.mean(jnp.square(o - mu), axis=-1, keepdims=True)
    y = ((o - mu) * lax.rsqrt(var + LNX_EPS)).reshape(o.shape[0], o.shape[1], -1)
    return y * w.astype(jnp.float32) + b.astype(jnp.float32)


def _rwkv7_scan(S0, r, w, k, v, kk, a):
    def step(S, inp):
        r_t, w_t, k_t, v_t, kk_t, a_t = inp
        sa = jnp.einsum('bhvk,bhk->bhv', S, -kk_t)
        S = (S * w_t[:, :, None, :] + sa[..., None] * (kk_t * a_t)[:, :, None, :]
             + v_t[..., None] * k_t[:, :, None, :])
        return S, jnp.einsum('bhvk,bhk->bhv', S, r_t)
    xs = tuple(jnp.swapaxes(t, 0, 1) for t in (r, w, k, v, kk, a))
    S, ys = lax.scan(step, S0.astype(jnp.float32), xs)
    return S, jnp.swapaxes(ys, 0, 1)


def _rwkv7(prm, v_first, S0, l, W):
    B, T, _ = prm.shape
    x = prm.astype(jnp.float32)
    r = x[..., 0:C_RWKV]
    k = x[..., C_RWKV:2 * C_RWKV]
    v = x[..., 2 * C_RWKV:3 * C_RWKV]
    o = 3 * C_RWKV
    wl = x[..., o:o + D_DECAY]
    o += D_DECAY
    al = x[..., o:o + D_AAA]
    o += D_AAA
    gl = x[..., o:o + D_GATE]
    w_raw = -jax.nn.softplus(-(W['w0'][l] + jnp.tanh(wl) @ W['w2'][l])) - 0.5
    decay = jnp.exp(-jnp.exp(w_raw.astype(jnp.float32)))
    a = jax.nn.sigmoid(W['a0'][l] + al @ W['a2'][l]).astype(jnp.float32)
    g = (jax.nn.sigmoid(gl) @ W['g2'][l]).astype(jnp.float32)
    if v_first is None:
        v_first = v
    else:
        j = l - 1
        v = v + (v_first - v) * jax.nn.sigmoid(W['v0'][j] + (v @ W['v1'][j]) @ W['v2'][j])
        v = v.astype(jnp.float32)
    hs = lambda t: t.reshape(B, T, H_RWKV, HEAD_DIM)
    kk = hs(k * W['k_k'][l]).astype(jnp.float32)
    kk = kk / jnp.maximum(jnp.sqrt(jnp.sum(kk * kk, axis=-1, keepdims=True)), 1e-12)
    k = (k * (1.0 + (a - 1.0) * W['k_a'][l])).astype(jnp.float32)
    S, y = _rwkv7_scan(S0, hs(r), hs(decay), hs(k), hs(v), kk, hs(a))
    y = _head_groupnorm(y, W['ln_x_w'][l], W['ln_x_b'][l])
    bonus = jnp.sum(hs(r) * hs(k) * W['r_k'][l].astype(jnp.float32), axis=-1, keepdims=True) * hs(v)
    out = (y + bonus.reshape(B, T, C_RWKV)) * g
    return out, v_first, S


def _sink_softmax(s, mask, sink):
    s = jnp.where(mask, s, NEG)
    sk = sink.astype(jnp.float32).reshape(KV_HEADS, GQA_GROUP, 1, 1)
    m = jnp.maximum(jnp.max(s, axis=-1, keepdims=True), sk)
    p = jnp.exp(s - m)
    return p / (jnp.sum(p, axis=-1, keepdims=True) + jnp.exp(sk - m))


def _swa_prompt(q, k, v, sink):
    B, T = q.shape[:2]
    L = ATTN_BLOCK
    NB = T // L
    qb = q.reshape(B, NB, L, KV_HEADS, GQA_GROUP, HEAD_DIM)
    def with_prev(t):
        cur = t.reshape(B, NB, L, KV_HEADS, HEAD_DIM)
        prev = jnp.concatenate([jnp.zeros_like(cur[:, :1]), cur[:, :-1]], axis=1)
        return jnp.concatenate([prev, cur], axis=2)
    kb, vb = with_prev(k), with_prev(v)
    qpos = jnp.arange(NB)[:, None] * L + jnp.arange(L)[None, :]
    kpos = (jnp.arange(NB)[:, None] - 1) * L + jnp.arange(2 * L)[None, :]
    diff = qpos[:, :, None] - kpos[:, None, :]
    mask = (diff >= 0) & (diff < WINDOW) & (kpos[:, None, :] >= 0)
    s = jnp.einsum('bnqhgd,bnshd->bnhgqs', qb, kb, preferred_element_type=jnp.float32) * (HEAD_DIM ** -0.5)
    p = _sink_softmax(s, mask[None, :, None, None], sink)
    o = jnp.einsum('bnhgqs,bnshd->bnqhgd', p.astype(v.dtype), vb)
    return o.reshape(B, T, C_ATTN), k[:, -WINDOW:], v[:, -WINDOW:]


def _swa_sample(q, k, v, ck, cv, sink):
    B, T = q.shape[:2]
    qg = q.reshape(B, T, KV_HEADS, GQA_GROUP, HEAD_DIM)
    kc = jnp.concatenate([ck.astype(k.dtype), k], axis=1)
    vc = jnp.concatenate([cv.astype(v.dtype), v], axis=1)
    qpos = PAST_LEN + jnp.arange(T)
    kpos = jnp.concatenate([PAST_LEN - WINDOW + jnp.arange(WINDOW), qpos])
    diff = qpos[:, None] - kpos[None, :]
    mask = (diff >= 0) & (diff < WINDOW) & (kpos[None, :] >= 0)
    s = jnp.einsum('bqhgd,bshd->bhgqs', qg, kc, preferred_element_type=jnp.float32) * (HEAD_DIM ** -0.5)
    p = _sink_softmax(s, mask[None, None, None], sink)
    o = jnp.einsum('bhgqs,bshd->bqhgd', p.astype(vc.dtype), vc)
    return o.reshape(B, T, C_ATTN), kc[:, -WINDOW:], vc[:, -WINDOW:]


def _time_mix(h, shift_prev, S0, v_first, l, W, attend):
    B, T, _ = h.shape
    proj = h @ W['w_in'][l]
    pr = proj[..., :N_SHIFT]
    prev = jnp.concatenate([shift_prev[:, None].astype(pr.dtype), pr[:, :-1]], axis=1)
    prm = pr + (prev - pr) * W['mu_shift'][l]
    o_r, v_first, S = _rwkv7(prm, v_first, S0, l, W)
    qa = proj[..., N_SHIFT:]
    nkv = KV_HEADS * HEAD_DIM
    q = qa[..., :C_ATTN].reshape(B, T, H_ATTN, HEAD_DIM)
    k = qa[..., C_ATTN:C_ATTN + nkv].reshape(B, T, KV_HEADS, HEAD_DIM)
    v = qa[..., C_ATTN + nkv:].reshape(B, T, KV_HEADS, HEAD_DIM)
    q = _rmsnorm(q, W['q_norm'][l])
    k = _rmsnorm(k, W['k_norm'][l])
    o_a, kbuf, vbuf = attend(l, q, k, v, W['sinks'][l])
    merged = jnp.concatenate([o_r.astype(h.dtype), o_a.astype(h.dtype)], axis=-1)
    return merged @ W['w_out'][l], v_first, pr[:, -1], S, kbuf, vbuf


def _swiglu(h, wg, wu, wd):
    return (jax.nn.silu(h @ wg) * (h @ wu)) @ wd


def _moe(h, j, W):
    B, T, D = h.shape
    M = B * T
    A = M * TOP_K
    x2 = h.reshape(M, D)
    logits = (x2 @ W['w_router'][j]).astype(jnp.float32) + W['b_router'][j].astype(jnp.float32)
    top_v, top_i = lax.top_k(logits, TOP_K)
    gates = jax.nn.softmax(top_v, axis=-1)
    e_flat = top_i.reshape(A)
    tok_flat = jnp.broadcast_to(jnp.arange(M, dtype=jnp.int32)[:, None], (M, TOP_K)).reshape(A)
    g_flat = gates.reshape(A)
    order = jnp.argsort(e_flat)
    e_sorted = e_flat[order]
    counts = jnp.bincount(e_flat, length=N_EXPERTS)
    padded = (counts + MOE_BLOCK - 1) // MOE_BLOCK * MOE_BLOCK
    pad_end = jnp.cumsum(padded)
    pad_start = pad_end - padded
    grp_start = jnp.cumsum(counts) - counts
    slot = pad_start[e_sorted] + jnp.arange(A) - grp_start[e_sorted]
    n_blocks = -(-(A + N_EXPERTS * (MOE_BLOCK - 1)) // MOE_BLOCK)
    P = n_blocks * MOE_BLOCK
    slot_tok = jnp.zeros((P,), jnp.int32).at[slot].set(tok_flat[order])
    slot_gate = jnp.zeros((P,), jnp.float32).at[slot].set(g_flat[order])
    block_e = jnp.minimum(jnp.searchsorted(pad_end, jnp.arange(n_blocks) * MOE_BLOCK, side='right'), N_EXPERTS - 1)
    wg, wu, wd = W['w_gate_e'][j], W['w_up_e'][j], W['w_down_e'][j]
    def block_fn(args):
        tok, e = args
        xb = x2[tok]
        return (jax.nn.silu(xb @ wg[e]) * (xb @ wu[e])) @ wd[e]
    yb = lax.map(block_fn, (slot_tok.reshape(n_blocks, MOE_BLOCK), block_e))
    yb = yb.reshape(P, D) * slot_gate[:, None].astype(yb.dtype)
    y = jax.ops.segment_sum(yb, slot_tok, num_segments=M)
    return y.reshape(B, T, D)


def _trunk(x, shift0, wkv0, attend, W):
    v_first = None
    shifts, states, kbufs, vbufs = [], [], [], []
    for l in range(DEPTH):
        h = _rmsnorm(x, W['norm_mix'][l])
        mixed, v_first, sh, S, kb, vb = _time_mix(h, shift0[l], wkv0[l], v_first, l, W, attend)
        x = x + mixed.astype(x.dtype)
        h = _rmsnorm(x, W['norm_ffn'][l])
        j = l // 2
        if l % 2 == 0:
            ff = _swiglu(h, W['w_gate_d'][j], W['w_up_d'][j], W['w_down_d'][j])
        else:
            ff = _moe(h, j, W)
        x = x + ff.astype(x.dtype)
        shifts.append(sh)
        states.append(S)
        kbufs.append(kb)
        vbufs.append(vb)
    return x, jnp.stack(states), jnp.stack(shifts), jnp.stack(kbufs), jnp.stack(vbufs)


def setup_inputs(seed: int = 0) -> dict:
    key = jax.random.key(seed)
    keys = jax.random.split(key, 64)
    cnt = [0]
    def nk():
        cnt[0] += 1
        return keys[cnt[0] - 1]
    nrm = lambda shape, scale: jax.random.normal(nk(), shape, jnp.float32) * scale
    uni = lambda shape, lo, hi: jax.random.uniform(nk(), shape, jnp.float32, lo, hi)
    n_dense = (DEPTH + 1) // 2
    n_moe = DEPTH // 2
    D = D_MODEL
    return {
        'x_prompt': nrm((BATCH, SEQ, D), 1.0),
        'x_sample': nrm((DEC_BATCH, DEC_SEQ, D), 1.0),
        'state_wkv': nrm((DEPTH, DEC_BATCH, H_RWKV, HEAD_DIM, HEAD_DIM), 0.3),
        'state_shift': nrm((DEPTH, DEC_BATCH, N_SHIFT), 1.0),
        'cache_swa_k': nrm((DEPTH, DEC_BATCH, WINDOW, KV_HEADS, HEAD_DIM), 1.0),
        'cache_swa_v': nrm((DEPTH, DEC_BATCH, WINDOW, KV_HEADS, HEAD_DIM), 1.0),
        'norm_mix': 1.0 + nrm((DEPTH, D), 0.02),
        'norm_ffn': 1.0 + nrm((DEPTH, D), 0.02),
        'w_in': nrm((DEPTH, D, N_IN), D ** -0.5),
        'mu_shift': uni((DEPTH, N_SHIFT), 0.0, 1.0),
        'w0': uni((DEPTH, C_RWKV), -2.5, 0.5),
        'w2': nrm((DEPTH, D_DECAY, C_RWKV), 0.1),
        'a0': nrm((DEPTH, C_RWKV), 0.5),
        'a2': nrm((DEPTH, D_AAA, C_RWKV), 0.1),
        'g2': nrm((DEPTH, D_GATE, C_RWKV), D_GATE ** -0.5),
        'k_k': 0.85 + nrm((DEPTH, C_RWKV), 0.05),
        'k_a': 1.0 + nrm((DEPTH, C_RWKV), 0.05),
        'r_k': nrm((DEPTH, H_RWKV, HEAD_DIM), 0.1),
        'ln_x_w': 1.0 + nrm((DEPTH, C_RWKV), 0.02),
        'ln_x_b': nrm((DEPTH, C_RWKV), 0.01),
        'v0': 1.0 + nrm((DEPTH - 1, C_RWKV), 0.1),
        'v1': nrm((DEPTH - 1, C_RWKV, D_MV), C_RWKV ** -0.5),
        'v2': nrm((DEPTH - 1, D_MV, C_RWKV), 0.1),
        'q_norm': 1.0 + nrm((DEPTH, HEAD_DIM), 0.02),
        'k_norm': 1.0 + nrm((DEPTH, HEAD_DIM), 0.02),
        'sinks': nrm((DEPTH, H_ATTN), 0.5),
        'w_out': nrm((DEPTH, D, D), D ** -0.5),
        'w_gate_d': nrm((n_dense, D, D_FF), D ** -0.5),
        'w_up_d': nrm((n_dense, D, D_FF), D ** -0.5),
        'w_down_d': nrm((n_dense, D_FF, D), D_FF ** -0.5),
        'w_router': nrm((n_moe, D, N_EXPERTS), D ** -0.5),
        'b_router': nrm((n_moe, N_EXPERTS), 0.01),
        'w_gate_e': nrm((n_moe, N_EXPERTS, D, D_FF), D ** -0.5),
        'w_up_e': nrm((n_moe, N_EXPERTS, D, D_FF), D ** -0.5),
        'w_down_e': nrm((n_moe, N_EXPERTS, D_FF, D), D_FF ** -0.5),
    }


def reference(x_prompt, x_sample, state_wkv, state_shift, cache_swa_k, cache_swa_v,
              norm_mix, norm_ffn, w_in, mu_shift, w0, w2, a0, a2, g2, k_k, k_a, r_k,
              ln_x_w, ln_x_b, v0, v1, v2, q_norm, k_norm, sinks, w_out,
              w_gate_d, w_up_d, w_down_d, w_router, b_router, w_gate_e, w_up_e, w_down_e):
    W = dict(norm_mix=norm_mix, norm_ffn=norm_ffn, w_in=w_in, mu_shift=mu_shift, w0=w0, w2=w2,
             a0=a0, a2=a2, g2=g2, k_k=k_k, k_a=k_a, r_k=r_k, ln_x_w=ln_x_w, ln_x_b=ln_x_b,
             v0=v0, v1=v1, v2=v2, q_norm=q_norm, k_norm=k_norm, sinks=sinks, w_out=w_out,
             w_gate_d=w_gate_d, w_up_d=w_up_d, w_down_d=w_down_d, w_router=w_router,
             b_router=b_router, w_gate_e=w_gate_e, w_up_e=w_up_e, w_down_e=w_down_e)
    shift0 = jnp.zeros((DEPTH, x_prompt.shape[0], N_SHIFT), x_prompt.dtype)
    wkv0 = jnp.zeros((DEPTH, x_prompt.shape[0], H_RWKV, HEAD_DIM, HEAD_DIM), jnp.float32)
    attend_prompt = lambda l, q, k, v, s: _swa_prompt(q, k, v, s)
    y_prompt, wkv_p, shift_p, k_p, v_p = _trunk(x_prompt, shift0, wkv0, attend_prompt, W)
    attend_sample = lambda l, q, k, v, s: _swa_sample(q, k, v, cache_swa_k[l], cache_swa_v[l], s)
    y_sample, wkv_s, shift_s, k_s, v_s = _trunk(x_sample, state_shift, state_wkv, attend_sample, W)
    return (y_prompt, y_sample, wkv_p, wkv_s, shift_p, shift_s, k_p, v_p, k_s, v_s)
```

```python
import functools

import jax
import jax.numpy as jnp
from jax import lax
from jax.experimental import pallas as pl
from jax.experimental.pallas import tpu as pltpu

F32 = jnp.float32
BF16 = jnp.bfloat16

HEAD_DIM = 64
C_RWKV = 512
H_RWKV = C_RWKV // HEAD_DIM
C_ATTN = 512
H_ATTN = C_ATTN // HEAD_DIM
KV_HEADS = 2
GQA_GROUP = H_ATTN // KV_HEADS
C_KV = KV_HEADS * HEAD_DIM
WINDOW = 128
ATTN_BLOCK = 128
D_DECAY = 64
D_AAA = 64
D_GATE = 128
N_SHIFT = 3 * C_RWKV + D_DECAY + D_AAA + D_GATE
N_QKV = C_ATTN + 2 * C_KV
N_EXPERTS = 8
TOP_K = 2
RMS_EPS = 1e-6
LNX_EPS = 64e-5
NEG = -1e30
LANES = 128
FF_CHUNK = 256
VMEM_LIMIT = 56 * 1024 * 1024


def _cparams(*sem):
    return pltpu.CompilerParams(dimension_semantics=sem, vmem_limit_bytes=VMEM_LIMIT)


def _pick_tile(n, prefs):
    for t in prefs:
        if n % t == 0:
            return t
    return n


def _const_spec(shape):
    nd = len(shape)
    return pl.BlockSpec(shape, lambda *_: (0,) * nd)


def _split2(x):
    hi = x.astype(BF16)
    lo = (x - hi.astype(F32)).astype(BF16)
    return hi, lo


def _dot(a, b):
    return jnp.dot(a, b, preferred_element_type=F32)


def _dot_nt(a, b):
    return lax.dot_general(a, b, (((1,), (1,)), ((), ())), preferred_element_type=F32)


def _dot_tn(a, b):
    return lax.dot_general(a, b, (((0,), (0,)), ((), ())), preferred_element_type=F32)


def _mm3(a, b, dot=_dot):
    ah, al = _split2(a)
    bh, bl = _split2(b)
    return dot(ah, bh) + dot(ah, bl) + dot(al, bh)


def _head_sum(x, ones_ref):
    ones = ones_ref[...]
    outs = []
    for c in range(x.shape[1] // LANES):
        hi, lo = _split2(x[:, c * LANES:(c + 1) * LANES])
        outs.append(_dot(hi, ones) + _dot(lo, ones))
    return outs[0] if len(outs) == 1 else jnp.concatenate(outs, axis=1)


def _sigmoid(x):
    return 1.0 / (1.0 + jnp.exp(-x))


def _rms_rows(x, g):
    return x * lax.rsqrt(jnp.mean(x * x, axis=-1, keepdims=True) + RMS_EPS) * g


def _norm_proj_kernel(x_ref, g_ref, w_ref, pr_ref, qa_ref):
    h = _rms_rows(x_ref[...], g_ref[...])
    y = _dot(h.astype(BF16), w_ref[...])
    pr_ref[...] = y[:, :N_SHIFT]
    qa_ref[...] = y[:, N_SHIFT:]


def _norm_proj(x, g, w_bf16):
    m, d = x.shape
    n_in = w_bf16.shape[1]
    tm = _pick_tile(m, (512, 256, 128, 64, 32, 16, 8))
    return pl.pallas_call(
        _norm_proj_kernel,
        out_shape=(jax.ShapeDtypeStruct((m, N_SHIFT), F32), jax.ShapeDtypeStruct((m, N_QKV), F32)),
        grid=(m // tm,),
        in_specs=[pl.BlockSpec((tm, d), lambda i: (i, 0)), _const_spec((1, d)), _const_spec((d, n_in))],
        out_specs=(pl.BlockSpec((tm, N_SHIFT), lambda i: (i, 0)), pl.BlockSpec((tm, N_QKV), lambda i: (i, 0))),
        compiler_params=_cparams("parallel"),
        name="norm_proj",
    )(x, g.reshape(1, d), w_bf16)


def _rwkv_prep_kernel(*refs, has_vfirst):
    if has_vfirst:
        (pr_ref, shift_ref, mu_ref, w0_ref, wwa_ref, a0_ref, g2_ref, kk_ref, ka_ref, rk_ref, ones_ref,
         vf_ref, v0_ref, v1_ref, v2_ref,
         r_out, lw_out, k_out, v_out, an_out, b_out, g_out, bonus_out, carry_ref) = refs
    else:
        (pr_ref, shift_ref, mu_ref, w0_ref, wwa_ref, a0_ref, g2_ref, kk_ref, ka_ref, rk_ref, ones_ref,
         r_out, lw_out, k_out, v_out, an_out, b_out, g_out, bonus_out, carry_ref) = refs
    bs, tt, n = pr_ref.shape
    rows = bs * tt
    j = pl.program_id(1)

    x3 = pr_ref[...]
    first3 = jnp.where(j == 0, shift_ref[...], carry_ref[...])
    carry_ref[...] = x3[:, tt - 1:tt, :]
    x = x3.reshape(rows, n)
    first = jnp.broadcast_to(first3, (bs, tt, n)).reshape(rows, n)
    rolled = pltpu.roll(x, 1, 0)
    tpos = lax.broadcasted_iota(jnp.int32, (rows, n), 0) % tt
    prev = jnp.where(tpos == 0, first, rolled)
    prm = x + (prev - x) * mu_ref[...]

    r = prm[:, 0:C_RWKV]
    k = prm[:, C_RWKV:2 * C_RWKV]
    v = prm[:, 2 * C_RWKV:3 * C_RWKV]
    wa = prm[:, 3 * C_RWKV:3 * C_RWKV + LANES]
    gl = prm[:, 3 * C_RWKV + LANES:3 * C_RWKV + 2 * LANES]

    lane = lax.broadcasted_iota(jnp.int32, wa.shape, 1)
    wa_in = jnp.where(lane < D_DECAY, jnp.tanh(wa), wa).astype(BF16)
    wa_out = _dot(wa_in, wwa_ref[...])
    z = w0_ref[...] + wa_out[:, :C_RWKV]
    softplus = jnp.maximum(-z, 0.0) + jnp.log(1.0 + jnp.exp(-jnp.abs(z)))
    lw = -jnp.exp(-softplus - 0.5)
    a = _sigmoid(a0_ref[...] + wa_out[:, C_RWKV:])
    g = _dot(_sigmoid(gl).astype(BF16), g2_ref[...])

    if has_vfirst:
        vf = vf_ref[...].reshape(rows, C_RWKV)
        t1 = _dot(v.astype(BF16), v1_ref[...])
        t2 = _dot(t1.astype(BF16), v2_ref[...])
        v = v + (vf - v) * _sigmoid(v0_ref[...] + t2)

    kk = k * kk_ref[...]
    norm = jnp.maximum(jnp.sqrt(_head_sum(kk * kk, ones_ref)), 1e-12)
    kk = kk / norm
    k = k * (1.0 + (a - 1.0) * ka_ref[...])
    bonus = _head_sum(r * k * rk_ref[...], ones_ref) * v

    shp = (bs, tt, C_RWKV)
    r_out[...] = r.reshape(shp)
    lw_out[...] = lw.reshape(shp)
    k_out[...] = k.reshape(shp)
    v_out[...] = v.reshape(shp)
    an_out[...] = (-kk).reshape(shp)
    b_out[...] = (kk * a).reshape(shp)
    g_out[...] = g.reshape(shp)
    bonus_out[...] = bonus.reshape(shp)


def _rwkv_prep(pr3, shift, lw_params, vfirst, ones128):
    nseq, t, n = pr3.shape
    if t >= 64:
        bs, tt = 1, _pick_tile(t, (256, 128, 64))
    else:
        bs, tt = _pick_tile(nseq, (16, 8, 4, 2, 1)), t
    p = lw_params
    ins = [pr3, shift.reshape(nseq, 1, n), p["mu"], p["w0"], p["wwa"], p["a0"], p["g2"], p["k_k"], p["k_a"],
           p["r_k"], ones128]
    seq_spec = lambda w: pl.BlockSpec((bs, tt, w), lambda i, j: (i, j, 0))
    specs = [seq_spec(n), pl.BlockSpec((bs, 1, n), lambda i, j: (i, 0, 0))] + [_const_spec(a.shape) for a in ins[2:]]
    has_vfirst = vfirst is not None
    if has_vfirst:
        extra = [vfirst, p["v0"], p["v1"], p["v2"]]
        ins += extra
        specs += [seq_spec(C_RWKV)] + [_const_spec(a.shape) for a in extra[1:]]
    out_sds = jax.ShapeDtypeStruct((nseq, t, C_RWKV), F32)
    return pl.pallas_call(
        functools.partial(_rwkv_prep_kernel, has_vfirst=has_vfirst),
        out_shape=(out_sds,) * 8,
        grid=(nseq // bs, t // tt),
        in_specs=specs,
        out_specs=(seq_spec(C_RWKV),) * 8,
        scratch_shapes=[pltpu.VMEM((bs, 1, n), F32)],
        compiler_params=_cparams("parallel", "arbitrary"),
        name="rwkv_prep",
    )(*ins)


def _scan_kernel(r_ref, lw_ref, k_ref, v_ref, an_ref, b_ref, s0_ref, y_ref, sout_ref, s_scr, *, chunk):
    c = pl.program_id(1)
    C = chunk

    @pl.when(c == 0)
    def _():
        s_scr[...] = s0_ref[0]

    lw = lw_ref[0]
    row = lax.broadcasted_iota(jnp.int32, (C, C), 0)
    col = lax.broadcasted_iota(jnp.int32, (C, C), 1)
    incl = row >= col
    strict = row > col
    tri = jnp.where(incl, 1.0, 0.0).astype(BF16)
    l1 = lw.astype(BF16)
    rem = lw - l1.astype(F32)
    l2 = rem.astype(BF16)
    l3 = (rem - l2.astype(F32)).astype(BF16)
    cum = _dot(tri, l1) + _dot(tri, l2) + _dot(tri, l3)
    cum_last = cum[C - 1:C, :]
    e_neg = jnp.exp(-cum)
    e_d = jnp.exp(cum_last - cum)
    an_p = an_ref[0] * jnp.exp(cum - lw)
    r_p = r_ref[0] * jnp.exp(cum)
    bq = b_ref[0]
    kq = k_ref[0]
    b_q = bq * e_neg
    k_q = kq * e_neg
    b_d = bq * e_d
    k_d = kq * e_d
    pc = jnp.exp(cum_last)
    v_all = v_ref[0]
    eye = jnp.where(row == col, 1.0, 0.0)

    n_sq = max(C.bit_length() - 2, 0)
    for h in range(H_RWKV):
        sl = slice(h * HEAD_DIM, (h + 1) * HEAD_DIM)
        an_h, r_h, v_h = an_p[:, sl], r_p[:, sl], v_all[:, sl]
        lhs = jnp.concatenate([an_h, r_h], axis=0)
        m_b = _mm3(lhs, b_q[:, sl], _dot_nt)
        m_k = _mm3(lhs, k_q[:, sl], _dot_nt)
        a_ab = jnp.where(strict, m_b[:C], 0.0)
        a_rb = jnp.where(incl, m_b[C:], 0.0)
        a_ak = jnp.where(strict, m_k[:C], 0.0)
        a_rk = jnp.where(incl, m_k[C:], 0.0)
        t_inv = eye + a_ab
        a_pow = a_ab
        for _ in range(n_sq):
            a_pow = _mm3(a_pow, a_pow)
            t_inv = t_inv + _mm3(t_inv, a_pow)
        akv = _mm3(a_ak, v_h)
        w1 = _mm3(t_inv, an_h)
        u0 = _mm3(t_inv, akv)
        s = s_scr[h]
        x1 = _mm3(jnp.concatenate([w1, r_h], axis=0), s, _dot_nt)
        u = x1[:C] + u0
        y = x1[C:] + _mm3(a_rb, u) + _mm3(a_rk, v_h)
        s_new = s * pc[:, sl] + _mm3(u, b_d[:, sl], _dot_tn) + _mm3(v_h, k_d[:, sl], _dot_tn)
        s_scr[h] = s_new
        y_ref[0, :, sl] = y

    @pl.when(c == pl.num_programs(1) - 1)
    def _():
        sout_ref[0] = s_scr[...]


def _rwkv_scan(r, lw, k, v, an, b, s0):
    nseq, t, _ = r.shape
    chunk = _pick_tile(t, (64, 32, 16, 8))
    seq_spec = pl.BlockSpec((1, chunk, C_RWKV), lambda i, j: (i, j, 0))
    st_spec = pl.BlockSpec((1, H_RWKV, HEAD_DIM, HEAD_DIM), lambda i, j: (i, 0, 0, 0))
    return pl.pallas_call(
        functools.partial(_scan_kernel, chunk=chunk),
        out_shape=(jax.ShapeDtypeStruct((nseq, t, C_RWKV), F32),
                   jax.ShapeDtypeStruct((nseq, H_RWKV, HEAD_DIM, HEAD_DIM), F32)),
        grid=(nseq, t // chunk),
        in_specs=[seq_spec] * 6 + [st_spec],
        out_specs=(seq_spec, st_spec),
        scratch_shapes=[pltpu.VMEM((H_RWKV, HEAD_DIM, HEAD_DIM), F32)],
        compiler_params=_cparams("parallel", "arbitrary"),
        name="rwkv_scan",
    )(r, lw, k, v, an, b, s0)


def _qk_norm(x, g, ones_ref):
    ms = _head_sum(x * x, ones_ref) * (1.0 / HEAD_DIM)
    return x * lax.rsqrt(ms + RMS_EPS) * g


def _sink_softmax_rows(s_parts, sink):
    m = sink
    for s in s_parts:
        m = jnp.maximum(m, jnp.max(s, axis=-1, keepdims=True))
    ps = [jnp.exp(s - m) for s in s_parts]
    den = jnp.exp(sink - m)
    for p in ps:
        den = den + jnp.sum(p, axis=-1, keepdims=True)
    inv = 1.0 / den
    return [p * inv for p in ps]


def _swa_prompt_kernel(sink_ref, cur_ref, prev_ref, qn_ref, kn_ref, ones_ref, o_ref, kc_ref):
    n = pl.program_id(1)
    L = ATTN_BLOCK
    cur = cur_ref[0]
    prev = prev_ref[0]
    q = _qk_norm(cur[:, :C_ATTN], qn_ref[...], ones_ref)
    k_cur = _qk_norm(cur[:, C_ATTN:C_ATTN + C_KV], kn_ref[...], ones_ref)
    k_prev = _qk_norm(prev[:, C_ATTN:C_ATTN + C_KV], kn_ref[...], ones_ref)
    kc_ref[0] = k_cur
    kk = jnp.concatenate([k_prev, k_cur], axis=0).astype(BF16)
    vv = jnp.concatenate([prev[:, C_ATTN + C_KV:], cur[:, C_ATTN + C_KV:]], axis=0).astype(BF16)
    qq = lax.broadcasted_iota(jnp.int32, (L, 2 * L), 0)
    jj = lax.broadcasted_iota(jnp.int32, (L, 2 * L), 1)
    mask = (jj > qq) & (jj <= qq + L) & ((jj >= L) | (n > 0))
    qb = q.astype(BF16)
    for h in range(H_ATTN):
        g = h // GQA_GROUP
        ksl = slice(g * HEAD_DIM, (g + 1) * HEAD_DIM)
        s = _dot_nt(qb[:, h * HEAD_DIM:(h + 1) * HEAD_DIM], kk[:, ksl]) * (HEAD_DIM ** -0.5)
        s = jnp.where(mask, s, NEG)
        (p,) = _sink_softmax_rows([s], sink_ref[h])
        o_ref[0, :, h * HEAD_DIM:(h + 1) * HEAD_DIM] = _dot(p.astype(BF16), vv[:, ksl])


def _swa_prompt(qa3, sinks, qn, kn, ones128):
    bsz, t, _ = qa3.shape
    L = ATTN_BLOCK
    nb = t // L
    return pl.pallas_call(
        _swa_prompt_kernel,
        out_shape=(jax.ShapeDtypeStruct((bsz, t, C_ATTN), F32), jax.ShapeDtypeStruct((bsz, L, C_KV), F32)),
        grid_spec=pltpu.PrefetchScalarGridSpec(
            num_scalar_prefetch=1,
            grid=(bsz, nb),
            in_specs=[pl.BlockSpec((1, L, N_QKV), lambda b, n, s: (b, n, 0)),
                      pl.BlockSpec((1, L, N_QKV), lambda b, n, s: (b, jnp.maximum(n - 1, 0), 0)),
                      pl.BlockSpec((1, C_ATTN), lambda b, n, s: (0, 0)),
                      pl.BlockSpec((1, C_KV), lambda b, n, s: (0, 0)),
                      pl.BlockSpec((LANES, LANES), lambda b, n, s: (0, 0))],
            out_specs=(pl.BlockSpec((1, L, C_ATTN), lambda b, n, s: (b, n, 0)),
                       pl.BlockSpec((1, L, C_KV), lambda b, n, s: (b, 0, 0)))),
        compiler_params=_cparams("parallel", "arbitrary"),
        name="swa_prompt",
    )(sinks, qa3, qa3, qn, kn, ones128)


def _swa_sample_kernel(sink_ref, qa_ref, ck_ref, cv_ref, qn_ref, kn_ref, ones_ref, o_ref, nk_ref, nv_ref, *, bb, t):
    W = WINDOW
    rows = GQA_GROUP * t
    rr = lax.broadcasted_iota(jnp.int32, (rows, W), 0) % t
    jc = lax.broadcasted_iota(jnp.int32, (rows, W), 1)
    mask_c = jc > rr
    rn = lax.broadcasted_iota(jnp.int32, (rows, t), 0) % t
    jn = lax.broadcasted_iota(jnp.int32, (rows, t), 1)
    mask_n = jn <= rn
    rowh = lax.broadcasted_iota(jnp.int32, (rows, 1), 0) // t

    def body(i, carry):
        x = qa_ref[i]
        q = _qk_norm(x[:, :C_ATTN], qn_ref[...], ones_ref)
        k_new = _qk_norm(x[:, C_ATTN:C_ATTN + C_KV], kn_ref[...], ones_ref)
        v_new = x[:, C_ATTN + C_KV:]
        ck = ck_ref[i]
        cv = cv_ref[i]
        nk_ref[i, 0:W - t, :] = ck[t:, :]
        nk_ref[i, W - t:W, :] = k_new
        nv_ref[i, 0:W - t, :] = cv[t:, :]
        nv_ref[i, W - t:W, :] = v_new
        outs = []
        for g in range(KV_HEADS):
            ksl = slice(g * HEAD_DIM, (g + 1) * HEAD_DIM)
            qg = jnp.concatenate(
                [q[:, (g * GQA_GROUP + a) * HEAD_DIM:(g * GQA_GROUP + a + 1) * HEAD_DIM] for a in range(GQA_GROUP)],
                axis=0).astype(BF16)
            sink = jnp.zeros((rows, 1), F32)
            for a in range(GQA_GROUP):
                sink = jnp.where(rowh == a, sink_ref[g * GQA_GROUP + a], sink)
            s_c = jnp.where(mask_c, _dot_nt(qg, ck[:, ksl].astype(BF16)) * (HEAD_DIM ** -0.5), NEG)
            s_n = jnp.where(mask_n, _dot_nt(qg, k_new[:, ksl].astype(BF16)) * (HEAD_DIM ** -0.5), NEG)
            p_c, p_n = _sink_softmax_rows([s_c, s_n], sink)
            og = _dot(p_c.astype(BF16), cv[:, ksl].astype(BF16)) + _dot(p_n.astype(BF16), v_new[:, ksl].astype(BF16))
            outs += [og[a * t:(a + 1) * t, :] for a in range(GQA_GROUP)]
        o_ref[i] = jnp.concatenate(outs, axis=1)
        return carry

    lax.fori_loop(0, bb, body, 0)


def _swa_sample(qa3, ck, cv, sinks, qn, kn, ones128):
    bsz, t, _ = qa3.shape
    W = WINDOW
    bb = _pick_tile(bsz, (16, 8, 4, 2, 1))
    sds = jax.ShapeDtypeStruct
    return pl.pallas_call(
        functools.partial(_swa_sample_kernel, bb=bb, t=t),
        out_shape=(sds((bsz, t, C_ATTN), F32), sds((bsz, W, C_KV), F32), sds((bsz, W, C_KV), F32)),
        grid_spec=pltpu.PrefetchScalarGridSpec(
            num_scalar_prefetch=1,
            grid=(bsz // bb,),
            in_specs=[pl.BlockSpec((bb, t, N_QKV), lambda b, s: (b, 0, 0)),
                      pl.BlockSpec((bb, W, C_KV), lambda b, s: (b, 0, 0)),
                      pl.BlockSpec((bb, W, C_KV), lambda b, s: (b, 0, 0)),
                      pl.BlockSpec((1, C_ATTN), lambda b, s: (0, 0)),
                      pl.BlockSpec((1, C_KV), lambda b, s: (0, 0)),
                      pl.BlockSpec((LANES, LANES), lambda b, s: (0, 0))],
            out_specs=(pl.BlockSpec((bb, t, C_ATTN), lambda b, s: (b, 0, 0)),
                       pl.BlockSpec((bb, W, C_KV), lambda b, s: (b, 0, 0)),
                       pl.BlockSpec((bb, W, C_KV), lambda b, s: (b, 0, 0)))),
        compiler_params=_cparams("parallel"),
        name="swa_sample",
    )(sinks, qa3, ck, cv, qn, kn, ones128)


def _post_kernel(*refs, has_router):
    if has_router:
        (x_ref, y_ref, bonus_ref, g_ref, oa_ref, lnw_ref, lnb_ref, wt_ref, wb_ref, gf_ref, ones_ref,
         wrh_ref, wrl_ref, br_ref, xmid_ref, h_ref, route_ref) = refs
    else:
        (x_ref, y_ref, bonus_ref, g_ref, oa_ref, lnw_ref, lnb_ref, wt_ref, wb_ref, gf_ref, ones_ref,
         xmid_ref, h_ref) = refs
    y = y_ref[...]
    mu = _head_sum(y, ones_ref) * (1.0 / HEAD_DIM)
    d = y - mu
    var = _head_sum(d * d, ones_ref) * (1.0 / HEAD_DIM)
    yn = d * lax.rsqrt(var + LNX_EPS) * lnw_ref[...] + lnb_ref[...]
    o_r = (yn + bonus_ref[...]) * g_ref[...]
    mixed = _dot(o_r.astype(BF16), wt_ref[...]) + _dot(oa_ref[...].astype(BF16), wb_ref[...])
    x_mid = x_ref[...] + mixed
    xmid_ref[...] = x_mid
    h = _rms_rows(x_mid, gf_ref[...])
    h_ref[...] = h.astype(h_ref.dtype)
    if has_router:
        hh, hl = _split2(h)
        logits = _dot(hh, wrh_ref[...]) + _dot(hh, wrl_ref[...]) + _dot(hl, wrh_ref[...]) + br_ref[...]
        lane = lax.broadcasted_iota(jnp.int32, logits.shape, 1)
        lg = jnp.where(lane < N_EXPERTS, logits, -jnp.inf)
        m1 = jnp.max(lg, axis=-1, keepdims=True)
        i1 = jnp.min(jnp.where(lg == m1, lane, LANES), axis=-1, keepdims=True)
        lg2 = jnp.where(lane == i1, -jnp.inf, lg)
        m2 = jnp.max(lg2, axis=-1, keepdims=True)
        i2 = jnp.min(jnp.where(lg2 == m2, lane, LANES), axis=-1, keepdims=True)
        e = jnp.exp(m2 - m1)
        g1 = 1.0 / (1.0 + e)
        g2 = e * g1
        route_ref[...] = jnp.where(lane == 0, i1.astype(F32),
                                   jnp.where(lane == 1, i2.astype(F32),
                                             jnp.where(lane == 2, g1, jnp.where(lane == 3, g2, 0.0))))


def _post(x, y, bonus, g, o_a, lnw, lnb, w_top, w_bot, g_ffn, ones128, router, h_dtype):
    m, d = x.shape
    tm = _pick_tile(m, (256, 128, 64, 32, 16, 8))
    row = lambda w: pl.BlockSpec((tm, w), lambda i: (i, 0))
    ins = [x, y, bonus, g, o_a, lnw, lnb, w_top, w_bot, g_ffn, ones128]
    specs = [row(d)] + [row(C_RWKV)] * 4 + [_const_spec(a.shape) for a in ins[5:]]
    outs = [jax.ShapeDtypeStruct((m, d), F32), jax.ShapeDtypeStruct((m, d), h_dtype)]
    out_specs = [row(d), row(d)]
    if router is not None:
        ins += list(router)
        specs += [_const_spec(a.shape) for a in router]
        outs.append(jax.ShapeDtypeStruct((m, LANES), F32))
        out_specs.append(row(LANES))
    return pl.pallas_call(
        functools.partial(_post_kernel, has_router=router is not None),
        out_shape=tuple(outs),
        grid=(m // tm,),
        in_specs=specs,
        out_specs=tuple(out_specs),
        compiler_params=_cparams("parallel"),
        name="post_mix",
    )(*ins)


def _ffn_kernel(*refs, has_res):
    if has_res:
        be_ref, x_ref, wg_ref, wu_ref, wd_ref, res_ref, o_ref = refs
    else:
        be_ref, x_ref, wg_ref, wu_ref, wd_ref, o_ref = refs
    del be_ref
    x = x_ref[...].astype(BF16)
    d_ff = wg_ref.shape[2]
    acc = None
    for f in range(d_ff // FF_CHUNK):
        fs = slice(f * FF_CHUNK, (f + 1) * FF_CHUNK)
        gt = _dot(x, wg_ref[0, :, fs])
        up = _dot(x, wu_ref[0, :, fs])
        act = (gt * _sigmoid(gt) * up).astype(BF16)
        part = _dot(act, wd_ref[0, fs, :])
        acc = part if acc is None else acc + part
    if has_res:
        acc = acc + res_ref[...]
    o_ref[...] = acc


def _ffn(x, block_e, wg, wu, wd, res, tm):
    m, d = x.shape
    d_ff = wg.shape[2]
    row = pl.BlockSpec((tm, d), lambda i, be: (i, 0))
    ins = [x, wg, wu, wd]
    specs = [row,
             pl.BlockSpec((1, d, d_ff), lambda i, be: (be[i], 0, 0)),
             pl.BlockSpec((1, d, d_ff), lambda i, be: (be[i], 0, 0)),
             pl.BlockSpec((1, d_ff, d), lambda i, be: (be[i], 0, 0))]
    if res is not None:
        ins.append(res)
        specs.append(row)
    return pl.pallas_call(
        functools.partial(_ffn_kernel, has_res=res is not None),
        out_shape=jax.ShapeDtypeStruct((m, d), F32),
        grid_spec=pltpu.PrefetchScalarGridSpec(
            num_scalar_prefetch=1, grid=(m // tm,), in_specs=specs, out_specs=row),
        compiler_params=_cparams("arbitrary"),
        name="swiglu",
    )(block_e, *ins)


def _gather_kernel(idx_ref, src_ref, o_ref, sem, *, rows):
    base = pl.program_id(0) * rows

    def start(r, carry):
        pltpu.make_async_copy(src_ref.at[pl.ds(idx_ref[base + r], 1)], o_ref.at[pl.ds(r, 1)], sem).start()
        return carry

    def wait(r, carry):
        pltpu.make_async_copy(src_ref.at[pl.ds(0, 1)], o_ref.at[pl.ds(r, 1)], sem).wait()
        return carry

    lax.fori_loop(0, rows, start, 0)
    lax.fori_loop(0, rows, wait, 0)


def _gather_rows(src, idx, rows):
    n = idx.shape[0]
    d = src.shape[1]
    return pl.pallas_call(
        functools.partial(_gather_kernel, rows=rows),
        out_shape=jax.ShapeDtypeStruct((n, d), src.dtype),
        grid_spec=pltpu.PrefetchScalarGridSpec(
            num_scalar_prefetch=1, grid=(n // rows,),
            in_specs=[pl.BlockSpec(memory_space=pl.ANY)],
            out_specs=pl.BlockSpec((rows, d), lambda i, idx: (i, 0)),
            scratch_shapes=[pltpu.SemaphoreType.DMA(())]),
        compiler_params=_cparams("arbitrary"),
        name="gather_rows",
    )(idx, src)


def _combine_kernel(s0_ref, s1_ref, x_ref, route_ref, yb_ref, o_ref, buf, sem, *, rows):
    base = pl.program_id(0) * rows

    def start(r, carry):
        pltpu.make_async_copy(yb_ref.at[pl.ds(s0_ref[base + r], 1)], buf.at[0, pl.ds(r, 1)], sem.at[0]).start()
        pltpu.make_async_copy(yb_ref.at[pl.ds(s1_ref[base + r], 1)], buf.at[1, pl.ds(r, 1)], sem.at[1]).start()
        return carry

    def wait(r, carry):
        pltpu.make_async_copy(yb_ref.at[pl.ds(0, 1)], buf.at[0, pl.ds(r, 1)], sem.at[0]).wait()
        pltpu.make_async_copy(yb_ref.at[pl.ds(0, 1)], buf.at[1, pl.ds(r, 1)], sem.at[1]).wait()
        return carry

    lax.fori_loop(0, rows, start, 0)
    lax.fori_loop(0, rows, wait, 0)
    route = route_ref[...]
    o_ref[...] = x_ref[...] + route[:, 2:3] * buf[0] + route[:, 3:4] * buf[1]


def _moe_combine(x_mid, route, yb, slot0, slot1):
    m, d = x_mid.shape
    rows = _pick_tile(m, (256, 128, 64, 32, 16, 8))
    row = lambda w: pl.BlockSpec((rows, w), lambda i, a, b: (i, 0))
    return pl.pallas_call(
        functools.partial(_combine_kernel, rows=rows),
        out_shape=jax.ShapeDtypeStruct((m, d), F32),
        grid_spec=pltpu.PrefetchScalarGridSpec(
            num_scalar_prefetch=2, grid=(m // rows,),
            in_specs=[row(d), row(LANES), pl.BlockSpec(memory_space=pl.ANY)],
            out_specs=row(d),
            scratch_shapes=[pltpu.VMEM((2, rows, d), F32), pltpu.SemaphoreType.DMA((2,))]),
        compiler_params=_cparams("arbitrary"),
        name="moe_combine",
    )(slot0, slot1, x_mid, route, yb)


def _moe_plan(route, tm):
    m = route.shape[0]
    a_tot = m * TOP_K
    e_flat = route[:, :TOP_K].astype(jnp.int32).reshape(a_tot)
    tok_flat = jnp.repeat(jnp.arange(m, dtype=jnp.int32), TOP_K)
    order = jnp.argsort(e_flat, stable=True)
    e_sorted = e_flat[order]
    counts = jnp.bincount(e_flat, length=N_EXPERTS)
    padded = (counts + tm - 1) // tm * tm
    pad_end = jnp.cumsum(padded)
    pad_start = pad_end - padded
    grp_start = jnp.cumsum(counts) - counts
    slot_sorted = (pad_start[e_sorted] + jnp.arange(a_tot) - grp_start[e_sorted]).astype(jnp.int32)
    n_blocks = -(-(a_tot + N_EXPERTS * (tm - 1)) // tm)
    slot_tok = jnp.zeros((n_blocks * tm,), jnp.int32).at[slot_sorted].set(tok_flat[order])
    block_e = jnp.minimum(jnp.searchsorted(pad_end, jnp.arange(n_blocks) * tm, side="right"),
                          N_EXPERTS - 1).astype(jnp.int32)
    slot_of = jnp.zeros((a_tot,), jnp.int32).at[order].set(slot_sorted).reshape(m, TOP_K)
    return slot_tok, block_e, slot_of[:, 0], slot_of[:, 1]


def _ones_block():
    i = jnp.arange(LANES) // HEAD_DIM
    return (i[:, None] == i[None, :]).astype(BF16)


def kernel(x_prompt, x_sample, state_wkv, state_shift, cache_swa_k, cache_swa_v, norm_mix, norm_ffn, w_in, mu_shift, w0, w2, a0, a2, g2, k_k, k_a, r_k, ln_x_w, ln_x_b, v0, v1, v2, q_norm, k_norm, sinks, w_out, w_gate_d, w_up_d, w_down_d, w_router, b_router, w_gate_e, w_up_e, w_down_e):
    depth = w_in.shape[0]
    bp, tp, d = x_prompt.shape
    bs, ts, _ = x_sample.shape
    mp, ms = bp * tp, bs * ts
    ones128 = _ones_block()
    x = jnp.concatenate([x_prompt.reshape(mp, d), x_sample.reshape(ms, d)], axis=0)
    row = lambda a: a.reshape(1, -1).astype(F32)

    shift_zero = jnp.zeros((bp, N_SHIFT), F32)
    wkv_zero = jnp.zeros((bp, H_RWKV, HEAD_DIM, HEAD_DIM), F32)
    vfirst_p = vfirst_s = None
    outs = {k_: [] for k_ in ("wkv_p", "wkv_s", "sh_p", "sh_s", "k_p", "v_p", "k_s", "v_s")}

    for l in range(depth):
        pr, qa = _norm_proj(x, norm_mix[l], w_in[l].astype(BF16))
        pr_p, pr_s = pr[:mp].reshape(bp, tp, N_SHIFT), pr[mp:].reshape(bs, ts, N_SHIFT)
        qa_p, qa_s = qa[:mp].reshape(bp, tp, N_QKV), qa[mp:].reshape(bs, ts, N_QKV)
        outs["sh_p"].append(pr_p[:, -1])
        outs["sh_s"].append(pr_s[:, -1])

        wwa = jnp.zeros((LANES, 2 * C_RWKV), F32)
        wwa = wwa.at[:D_DECAY, :C_RWKV].set(w2[l]).at[D_DECAY:, C_RWKV:].set(a2[l]).astype(BF16)
        prm = dict(mu=row(mu_shift[l]), w0=row(w0[l]), wwa=wwa, a0=row(a0[l]), g2=g2[l].astype(BF16),
                   k_k=row(k_k[l]), k_a=row(k_a[l]), r_k=row(r_k[l]))
        if l > 0:
            d_mv = v1.shape[2]
            prm["v0"] = row(v0[l - 1])
            prm["v1"] = jnp.zeros((C_RWKV, LANES), F32).at[:, :d_mv].set(v1[l - 1]).astype(BF16)
            prm["v2"] = jnp.zeros((LANES, C_RWKV), F32).at[:d_mv].set(v2[l - 1]).astype(BF16)
        r_p, lw_p, k_p, v_p, an_p, b_p, g_p, bonus_p = _rwkv_prep(pr_p, shift_zero, prm, vfirst_p, ones128)
        r_s, lw_s, k_s, v_s, an_s, b_s, g_s, bonus_s = _rwkv_prep(pr_s, state_shift[l], prm, vfirst_s, ones128)
        if l == 0:
            vfirst_p, vfirst_s = v_p, v_s

        y_p, wkv_p = _rwkv_scan(r_p, lw_p, k_p, v_p, an_p, b_p, wkv_zero)
        y_s, wkv_s = _rwkv_scan(r_s, lw_s, k_s, v_s, an_s, b_s, state_wkv[l])
        outs["wkv_p"].append(wkv_p)
        outs["wkv_s"].append(wkv_s)

        qn = jnp.tile(row(q_norm[l]), (1, H_ATTN))
        kn = jnp.tile(row(k_norm[l]), (1, KV_HEADS))
        sk = sinks[l].astype(F32)
        oa_p, kc_p = _swa_prompt(qa_p, sk, qn, kn, ones128)
        ck = cache_swa_k[l].reshape(bs, WINDOW, C_KV)
        cv = cache_swa_v[l].reshape(bs, WINDOW, C_KV)
        oa_s, nk_s, nv_s = _swa_sample(qa_s, ck, cv, sk, qn, kn, ones128)
        outs["k_p"].append(kc_p.reshape(bp, WINDOW, KV_HEADS, HEAD_DIM))
        outs["v_p"].append(qa_p[:, tp - WINDOW:, C_ATTN + C_KV:].reshape(bp, WINDOW, KV_HEADS, HEAD_DIM))
        outs["k_s"].append(nk_s.reshape(bs, WINDOW, KV_HEADS, HEAD_DIM))
        outs["v_s"].append(nv_s.reshape(bs, WINDOW, KV_HEADS, HEAD_DIM))

        cat = lambda a, b: jnp.concatenate([a.reshape(mp, -1), b.reshape(ms, -1)], axis=0)
        wo = w_out[l].astype(BF16)
        moe = l % 2 == 1
        j = l // 2
        router = None
        if moe:
            wr = jnp.zeros((d, LANES), F32).at[:, :N_EXPERTS].set(w_router[j])
            wr_hi = wr.astype(BF16)
            wr_lo = (wr - wr_hi.astype(F32)).astype(BF16)
            br = jnp.zeros((1, LANES), F32).at[0, :N_EXPERTS].set(b_router[j])
            router = (wr_hi, wr_lo, br)
        post = _post(x, cat(y_p, y_s), cat(bonus_p, bonus_s), cat(g_p, g_s), cat(oa_p, oa_s),
                     row(ln_x_w[l]), row(ln_x_b[l]), wo[:C_RWKV], wo[C_RWKV:], row(norm_ffn[l]), ones128,
                     router, F32 if moe else BF16)
        if not moe:
            x_mid, h = post
            tm = _pick_tile(mp + ms, (512, 256, 128, 64, 32, 16, 8))
            x = _ffn(h, jnp.zeros(((mp + ms) // tm,), jnp.int32), w_gate_d[j][None].astype(BF16),
                     w_up_d[j][None].astype(BF16), w_down_d[j][None].astype(BF16), x_mid, tm)
        else:
            x_mid, h, route = post
            tm = 256 if (mp + ms) >= 4096 else 16
            slot_tok, block_e, slot0, slot1 = _moe_plan(route, tm)
            xs = _gather_rows(h, slot_tok, tm)
            yb = _ffn(xs, block_e, w_gate_e[j].astype(BF16), w_up_e[j].astype(BF16), w_down_e[j].astype(BF16),
                      None, tm)
            x = _moe_combine(x_mid, route, yb, slot0, slot1)

    st = lambda name: jnp.stack(outs[name])
    return (x[:mp].reshape(bp, tp, d), x[mp:].reshape(bs, ts, d), st("wkv_p"), st("wkv_s"), st("sh_p"), st("sh_s"),
            st("k_p"), st("v_p"), st("k_s"), st("v_s"))
```

```python
import functools

import jax
import jax.numpy as jnp
from jax import lax
from jax.experimental import pallas as pl
from jax.experimental.pallas import tpu as pltpu

F32 = jnp.float32
BF16 = jnp.bfloat16

HEAD_DIM = 64
C_RWKV = 512
H_RWKV = C_RWKV // HEAD_DIM
C_ATTN = 512
H_ATTN = C_ATTN // HEAD_DIM
KV_HEADS = 2
GQA_GROUP = H_ATTN // KV_HEADS
C_KV = KV_HEADS * HEAD_DIM
WINDOW = 128
ATTN_BLOCK = 128
D_DECAY = 64
D_AAA = 64
D_GATE = 128
N_SHIFT = 3 * C_RWKV + D_DECAY + D_AAA + D_GATE
N_QKV = C_ATTN + 2 * C_KV
N_EXPERTS = 8
TOP_K = 2
RMS_EPS = 1e-6
LNX_EPS = 64e-5
NEG = -1e30
LANES = 128
FF_CHUNK = 256
VMEM_LIMIT = 56 * 1024 * 1024


def _cparams(*sem):
    return pltpu.CompilerParams(dimension_semantics=sem, vmem_limit_bytes=VMEM_LIMIT)


def _pick_tile(n, prefs):
    for t in prefs:
        if n % t == 0:
            return t
    return n


def _const_spec(shape):
    nd = len(shape)
    return pl.BlockSpec(shape, lambda *_: (0,) * nd)


def _split2(x):
    hi = x.astype(BF16)
    lo = (x - hi.astype(F32)).astype(BF16)
    return hi, lo


def _dot(a, b):
    return jnp.dot(a, b, preferred_element_type=F32)


def _dot_nt(a, b):
    return lax.dot_general(a, b, (((1,), (1,)), ((), ())), preferred_element_type=F32)


def _dot_tn(a, b):
    return lax.dot_general(a, b, (((0,), (0,)), ((), ())), preferred_element_type=F32)


def _mm3(a, b, dot=_dot):
    ah, al = _split2(a)
    bh, bl = _split2(b)
    return dot(ah, bh) + dot(ah, bl) + dot(al, bh)


def _head_sum(x, ones_ref):
    ones = ones_ref[...]
    outs = []
    for c in range(x.shape[1] // LANES):
        hi, lo = _split2(x[:, c * LANES:(c + 1) * LANES])
        outs.append(_dot(hi, ones) + _dot(lo, ones))
    return outs[0] if len(outs) == 1 else jnp.concatenate(outs, axis=1)


def _sigmoid(x):
    return 1.0 / (1.0 + jnp.exp(-x))


def _rms_rows(x, g):
    return x * lax.rsqrt(jnp.mean(x * x, axis=-1, keepdims=True) + RMS_EPS) * g


def _norm_proj_kernel(x_ref, g_ref, w_ref, pr_ref, qa_ref):
    h = _rms_rows(x_ref[...], g_ref[...])
    y = _dot(h.astype(BF16), w_ref[...])
    pr_ref[...] = y[:, :N_SHIFT]
    qa_ref[...] = y[:, N_SHIFT:]


def _norm_proj(x, g, w_bf16):
    m, d = x.shape
    n_in = w_bf16.shape[1]
    tm = _pick_tile(m, (512, 256, 128, 64, 32, 16, 8))
    return pl.pallas_call(
        _norm_proj_kernel,
        out_shape=(jax.ShapeDtypeStruct((m, N_SHIFT), F32), jax.ShapeDtypeStruct((m, N_QKV), F32)),
        grid=(m // tm,),
        in_specs=[pl.BlockSpec((tm, d), lambda i: (i, 0)), _const_spec((1, d)), _const_spec((d, n_in))],
        out_specs=(pl.BlockSpec((tm, N_SHIFT), lambda i: (i, 0)), pl.BlockSpec((tm, N_QKV), lambda i: (i, 0))),
        compiler_params=_cparams("parallel"),
        name="norm_proj",
    )(x, g.reshape(1, d), w_bf16)


def _rwkv_prep_kernel(*refs, has_vfirst):
    if has_vfirst:
        (pr_ref, shift_ref, mu_ref, w0_ref, wwa_ref, a0_ref, g2_ref, kk_ref, ka_ref, rk_ref, ones_ref,
         vf_ref, v0_ref, v1_ref, v2_ref,
         r_out, lw_out, k_out, v_out, an_out, b_out, g_out, bonus_out, carry_ref) = refs
    else:
        (pr_ref, shift_ref, mu_ref, w0_ref, wwa_ref, a0_ref, g2_ref, kk_ref, ka_ref, rk_ref, ones_ref,
         r_out, lw_out, k_out, v_out, an_out, b_out, g_out, bonus_out, carry_ref) = refs
    bs, tt, n = pr_ref.shape
    rows = bs * tt
    j = pl.program_id(1)

    x3 = pr_ref[...]
    first3 = jnp.where(j == 0, shift_ref[...], carry_ref[...])
    carry_ref[...] = x3[:, tt - 1:tt, :]
    x = x3.reshape(rows, n)
    first = jnp.broadcast_to(first3, (bs, tt, n)).reshape(rows, n)
    rolled = pltpu.roll(x, 1, 0)
    tpos = lax.broadcasted_iota(jnp.int32, (rows, n), 0) % tt
    prev = jnp.where(tpos == 0, first, rolled)
    prm = x + (prev - x) * mu_ref[...]

    r = prm[:, 0:C_RWKV]
    k = prm[:, C_RWKV:2 * C_RWKV]
    v = prm[:, 2 * C_RWKV:3 * C_RWKV]
    wa = prm[:, 3 * C_RWKV:3 * C_RWKV + LANES]
    gl = prm[:, 3 * C_RWKV + LANES:3 * C_RWKV + 2 * LANES]

    lane = lax.broadcasted_iota(jnp.int32, wa.shape, 1)
    wa_in = jnp.where(lane < D_DECAY, jnp.tanh(wa), wa).astype(BF16)
    wa_out = _dot(wa_in, wwa_ref[...])
    z = w0_ref[...] + wa_out[:, :C_RWKV]
    softplus = jnp.maximum(-z, 0.0) + jnp.log(1.0 + jnp.exp(-jnp.abs(z)))
    lw = -jnp.exp(-softplus - 0.5)
    a = _sigmoid(a0_ref[...] + wa_out[:, C_RWKV:])
    g = _dot(_sigmoid(gl).astype(BF16), g2_ref[...])

    if has_vfirst:
        vf = vf_ref[...].reshape(rows, C_RWKV)
        t1 = _dot(v.astype(BF16), v1_ref[...])
        t2 = _dot(t1.astype(BF16), v2_ref[...])
        v = v + (vf - v) * _sigmoid(v0_ref[...] + t2)

    kk = k * kk_ref[...]
    norm = jnp.maximum(jnp.sqrt(_head_sum(kk * kk, ones_ref)), 1e-12)
    kk = kk / norm
    k = k * (1.0 + (a - 1.0) * ka_ref[...])
    bonus = _head_sum(r * k * rk_ref[...], ones_ref) * v

    shp = (bs, tt, C_RWKV)
    r_out[...] = r.reshape(shp)
    lw_out[...] = lw.reshape(shp)
    k_out[...] = k.reshape(shp)
    v_out[...] = v.reshape(shp)
    an_out[...] = (-kk).reshape(shp)
    b_out[...] = (kk * a).reshape(shp)
    g_out[...] = g.reshape(shp)
    bonus_out[...] = bonus.reshape(shp)


def _rwkv_prep(pr3, shift, lw_params, vfirst, ones128):
    nseq, t, n = pr3.shape
    if t >= 64:
        bs, tt = 1, _pick_tile(t, (256, 128, 64))
    else:
        bs, tt = _pick_tile(nseq, (16, 8, 4, 2, 1)), t
    p = lw_params
    ins = [pr3, shift.reshape(nseq, 1, n), p["mu"], p["w0"], p["wwa"], p["a0"], p["g2"], p["k_k"], p["k_a"],
           p["r_k"], ones128]
    seq_spec = lambda w: pl.BlockSpec((bs, tt, w), lambda i, j: (i, j, 0))
    specs = [seq_spec(n), pl.BlockSpec((bs, 1, n), lambda i, j: (i, 0, 0))] + [_const_spec(a.shape) for a in ins[2:]]
    has_vfirst = vfirst is not None
    if has_vfirst:
        extra = [vfirst, p["v0"], p["v1"], p["v2"]]
        ins += extra
        specs += [seq_spec(C_RWKV)] + [_const_spec(a.shape) for a in extra[1:]]
    out_sds = jax.ShapeDtypeStruct((nseq, t, C_RWKV), F32)
    return pl.pallas_call(
        functools.partial(_rwkv_prep_kernel, has_vfirst=has_vfirst),
        out_shape=(out_sds,) * 8,
        grid=(nseq // bs, t // tt),
        in_specs=specs,
        out_specs=(seq_spec(C_RWKV),) * 8,
        scratch_shapes=[pltpu.VMEM((bs, 1, n), F32)],
        compiler_params=_cparams("parallel", "arbitrary"),
        name="rwkv_prep",
    )(*ins)


def _sp(x):
    return _split2(x)


def _mm3s(a, b, dot=_dot):
    return dot(a[0], b[0]) + dot(a[0], b[1]) + dot(a[1], b[0])


def _scan_chunk(r, lw, k, v, an, b, s_list, chunk):
    C = chunk
    H = range(H_RWKV)
    row = lax.broadcasted_iota(jnp.int32, (C, C), 0)
    col = lax.broadcasted_iota(jnp.int32, (C, C), 1)
    incl = row >= col
    strict = row > col
    tri = jnp.where(incl, 1.0, 0.0).astype(BF16)
    l1 = lw.astype(BF16)
    rem = lw - l1.astype(F32)
    l2 = rem.astype(BF16)
    l3 = (rem - l2.astype(F32)).astype(BF16)
    cum = _dot(tri, l1) + _dot(tri, l2) + _dot(tri, l3)
    cum_last = cum[C - 1:C, :]
    e_neg = jnp.exp(-cum)
    e_d = jnp.exp(cum_last - cum)
    an_p = an * jnp.exp(cum - lw)
    r_p = r * jnp.exp(cum)
    pc = jnp.exp(cum_last)
    eye = jnp.where(row == col, 1.0, 0.0)
    hs = lambda x, h: x[:, h * HEAD_DIM:(h + 1) * HEAD_DIM]

    lhs = _sp(jnp.concatenate([an_p, r_p], axis=0))
    bq_s, kq_s = _sp(b * e_neg), _sp(k * e_neg)
    bd_s, kd_s = _sp(b * e_d), _sp(k * e_d)
    v_s = _sp(v)
    hsp = lambda xs, h: (hs(xs[0], h), hs(xs[1], h))
    m_b = [_mm3s(hsp(lhs, h), hsp(bq_s, h), _dot_nt) for h in H]
    m_k = [_mm3s(hsp(lhs, h), hsp(kq_s, h), _dot_nt) for h in H]
    a_ab = [jnp.where(strict, m[:C], 0.0) for m in m_b]
    a_rb = [jnp.where(incl, m[C:], 0.0) for m in m_b]
    a_ak = [jnp.where(strict, m[:C], 0.0) for m in m_k]
    a_rk = [jnp.where(incl, m[C:], 0.0) for m in m_k]
    vh = [hsp(v_s, h) for h in H]
    akv = [_mm3s(_sp(a_ak[h]), vh[h]) for h in H]
    yk = [_mm3s(_sp(a_rk[h]), vh[h]) for h in H]
    skv = [_mm3s(vh[h], hsp(kd_s, h), _dot_tn) for h in H]

    t_inv = [eye + a for a in a_ab]
    n_sq = max(C.bit_length() - 2, 0)
    if n_sq:
        a_s = [_sp(a) for a in a_ab]
        p_pow = [_mm3s(a_s[h], a_s[h]) for h in H]
        for lev in range(n_sq):
            p_s = [_sp(p) for p in p_pow]
            t_inv = [t_inv[h] + _mm3s(_sp(t_inv[h]), p_s[h]) for h in H]
            if lev + 1 < n_sq:
                p_pow = [_mm3s(p_s[h], p_s[h]) for h in H]
    t_s = [_sp(t) for t in t_inv]
    lhs_an = (lhs[0][:C], lhs[1][:C])
    lhs_r = (lhs[0][C:], lhs[1][C:])
    w1 = [_mm3s(t_s[h], hsp(lhs_an, h)) for h in H]
    u0 = [_mm3s(t_s[h], _sp(akv[h])) for h in H]
    s_s = [_sp(s) for s in s_list]
    xw = [_mm3s(_sp(w1[h]), s_s[h], _dot_nt) for h in H]
    xr = [_mm3s(hsp(lhs_r, h), s_s[h], _dot_nt) for h in H]
    u = [xw[h] + u0[h] for h in H]
    u_s = [_sp(x) for x in u]
    y = [xr[h] + _mm3s(_sp(a_rb[h]), u_s[h]) + yk[h] for h in H]
    s_new = [s_list[h] * hs(pc, h) + _mm3s(u_s[h], hsp(bd_s, h), _dot_tn) + skv[h] for h in H]
    return jnp.concatenate(y, axis=1), s_new


def _scan_kernel(r_ref, lw_ref, k_ref, v_ref, an_ref, b_ref, s0_ref, y_ref, sout_ref, s_scr, *, chunk, nb):
    c = pl.program_id(1)

    @pl.when(c == 0)
    def _():
        s_scr[...] = s0_ref[...]

    def body(i, carry):
        s_list = [s_scr[i, h] for h in range(H_RWKV)]
        y, s_new = _scan_chunk(r_ref[i], lw_ref[i], k_ref[i], v_ref[i], an_ref[i], b_ref[i], s_list, chunk)
        y_ref[i] = y
        for h in range(H_RWKV):
            s_scr[i, h] = s_new[h]
        return carry

    if nb == 1:
        body(0, 0)
    else:
        lax.fori_loop(0, nb, body, 0)

    @pl.when(c == pl.num_programs(1) - 1)
    def _():
        sout_ref[...] = s_scr[...]


def _rwkv_scan(r, lw, k, v, an, b, s0):
    nseq, t, _ = r.shape
    chunk = _pick_tile(t, (64, 32, 16, 8))
    nb = 1 if t > chunk else _pick_tile(nseq, (8, 4, 2, 1))
    seq_spec = pl.BlockSpec((nb, chunk, C_RWKV), lambda i, j: (i, j, 0))
    st_spec = pl.BlockSpec((nb, H_RWKV, HEAD_DIM, HEAD_DIM), lambda i, j: (i, 0, 0, 0))
    return pl.pallas_call(
        functools.partial(_scan_kernel, chunk=chunk, nb=nb),
        out_shape=(jax.ShapeDtypeStruct((nseq, t, C_RWKV), F32),
                   jax.ShapeDtypeStruct((nseq, H_RWKV, HEAD_DIM, HEAD_DIM), F32)),
        grid=(nseq // nb, t // chunk),
        in_specs=[seq_spec] * 6 + [st_spec],
        out_specs=(seq_spec, st_spec),
        scratch_shapes=[pltpu.VMEM((nb, H_RWKV, HEAD_DIM, HEAD_DIM), F32)],
        compiler_params=_cparams("parallel", "arbitrary"),
        name="rwkv_scan",
    )(r, lw, k, v, an, b, s0)


def _qk_norm(x, g, ones_ref):
    ms = _head_sum(x * x, ones_ref) * (1.0 / HEAD_DIM)
    return x * lax.rsqrt(ms + RMS_EPS) * g


def _sink_softmax_rows(s_parts, sink):
    m = sink
    for s in s_parts:
        m = jnp.maximum(m, jnp.max(s, axis=-1, keepdims=True))
    ps = [jnp.exp(s - m) for s in s_parts]
    den = jnp.exp(sink - m)
    for p in ps:
        den = den + jnp.sum(p, axis=-1, keepdims=True)
    inv = 1.0 / den
    return [p * inv for p in ps]


def _swa_prompt_kernel(sink_ref, cur_ref, prev_ref, qn_ref, kn_ref, ones_ref, o_ref, kc_ref):
    n = pl.program_id(1)
    L = ATTN_BLOCK
    cur = cur_ref[0]
    prev = prev_ref[0]
    q = _qk_norm(cur[:, :C_ATTN], qn_ref[...], ones_ref)
    k_cur = _qk_norm(cur[:, C_ATTN:C_ATTN + C_KV], kn_ref[...], ones_ref)
    k_prev = _qk_norm(prev[:, C_ATTN:C_ATTN + C_KV], kn_ref[...], ones_ref)
    kc_ref[0] = k_cur
    kk = jnp.concatenate([k_prev, k_cur], axis=0).astype(BF16)
    vv = jnp.concatenate([prev[:, C_ATTN + C_KV:], cur[:, C_ATTN + C_KV:]], axis=0).astype(BF16)
    qq = lax.broadcasted_iota(jnp.int32, (L, 2 * L), 0)
    jj = lax.broadcasted_iota(jnp.int32, (L, 2 * L), 1)
    mask = (jj > qq) & (jj <= qq + L) & ((jj >= L) | (n > 0))
    qb = q.astype(BF16)
    for h in range(H_ATTN):
        g = h // GQA_GROUP
        ksl = slice(g * HEAD_DIM, (g + 1) * HEAD_DIM)
        s = _dot_nt(qb[:, h * HEAD_DIM:(h + 1) * HEAD_DIM], kk[:, ksl]) * (HEAD_DIM ** -0.5)
        s = jnp.where(mask, s, NEG)
        (p,) = _sink_softmax_rows([s], sink_ref[h])
        o_ref[0, :, h * HEAD_DIM:(h + 1) * HEAD_DIM] = _dot(p.astype(BF16), vv[:, ksl])


def _swa_prompt(qa3, sinks, qn, kn, ones128):
    bsz, t, _ = qa3.shape
    L = ATTN_BLOCK
    nb = t // L
    return pl.pallas_call(
        _swa_prompt_kernel,
        out_shape=(jax.ShapeDtypeStruct((bsz, t, C_ATTN), F32), jax.ShapeDtypeStruct((bsz, L, C_KV), F32)),
        grid_spec=pltpu.PrefetchScalarGridSpec(
            num_scalar_prefetch=1,
            grid=(bsz, nb),
            in_specs=[pl.BlockSpec((1, L, N_QKV), lambda b, n, s: (b, n, 0)),
                      pl.BlockSpec((1, L, N_QKV), lambda b, n, s: (b, jnp.maximum(n - 1, 0), 0)),
                      pl.BlockSpec((1, C_ATTN), lambda b, n, s: (0, 0)),
                      pl.BlockSpec((1, C_KV), lambda b, n, s: (0, 0)),
                      pl.BlockSpec((LANES, LANES), lambda b, n, s: (0, 0))],
            out_specs=(pl.BlockSpec((1, L, C_ATTN), lambda b, n, s: (b, n, 0)),
                       pl.BlockSpec((1, L, C_KV), lambda b, n, s: (b, 0, 0)))),
        compiler_params=_cparams("parallel", "arbitrary"),
        name="swa_prompt",
    )(sinks, qa3, qa3, qn, kn, ones128)


def _swa_sample_kernel(sink_ref, qa_ref, ck_ref, cv_ref, qn_ref, kn_ref, ones_ref, o_ref, nk_ref, nv_ref, *, bb, t):
    W = WINDOW
    rows = GQA_GROUP * t
    rr = lax.broadcasted_iota(jnp.int32, (rows, W), 0) % t
    jc = lax.broadcasted_iota(jnp.int32, (rows, W), 1)
    mask_c = jc > rr
    rn = lax.broadcasted_iota(jnp.int32, (rows, t), 0) % t
    jn = lax.broadcasted_iota(jnp.int32, (rows, t), 1)
    mask_n = jn <= rn
    rowh = lax.broadcasted_iota(jnp.int32, (rows, 1), 0) // t

    def body(i, carry):
        x = qa_ref[i]
        q = _qk_norm(x[:, :C_ATTN], qn_ref[...], ones_ref)
        k_new = _qk_norm(x[:, C_ATTN:C_ATTN + C_KV], kn_ref[...], ones_ref)
        v_new = x[:, C_ATTN + C_KV:]
        ck = ck_ref[i]
        cv = cv_ref[i]
        nk_ref[i, 0:W - t, :] = ck[t:, :]
        nk_ref[i, W - t:W, :] = k_new
        nv_ref[i, 0:W - t, :] = cv[t:, :]
        nv_ref[i, W - t:W, :] = v_new
        outs = []
        for g in range(KV_HEADS):
            ksl = slice(g * HEAD_DIM, (g + 1) * HEAD_DIM)
            qg = jnp.concatenate(
                [q[:, (g * GQA_GROUP + a) * HEAD_DIM:(g * GQA_GROUP + a + 1) * HEAD_DIM] for a in range(GQA_GROUP)],
                axis=0).astype(BF16)
            sink = jnp.zeros((rows, 1), F32)
            for a in range(GQA_GROUP):
                sink = jnp.where(rowh == a, sink_ref[g * GQA_GROUP + a], sink)
            s_c = jnp.where(mask_c, _dot_nt(qg, ck[:, ksl].astype(BF16)) * (HEAD_DIM ** -0.5), NEG)
            s_n = jnp.where(mask_n, _dot_nt(qg, k_new[:, ksl].astype(BF16)) * (HEAD_DIM ** -0.5), NEG)
            p_c, p_n = _sink_softmax_rows([s_c, s_n], sink)
            og = _dot(p_c.astype(BF16), cv[:, ksl].astype(BF16)) + _dot(p_n.astype(BF16), v_new[:, ksl].astype(BF16))
            outs += [og[a * t:(a + 1) * t, :] for a in range(GQA_GROUP)]
        o_ref[i] = jnp.concatenate(outs, axis=1)
        return carry

    lax.fori_loop(0, bb, body, 0)


def _swa_sample(qa3, ck, cv, sinks, qn, kn, ones128):
    bsz, t, _ = qa3.shape
    W = WINDOW
    bb = _pick_tile(bsz, (16, 8, 4, 2, 1))
    sds = jax.ShapeDtypeStruct
    return pl.pallas_call(
        functools.partial(_swa_sample_kernel, bb=bb, t=t),
        out_shape=(sds((bsz, t, C_ATTN), F32), sds((bsz, W, C_KV), F32), sds((bsz, W, C_KV), F32)),
        grid_spec=pltpu.PrefetchScalarGridSpec(
            num_scalar_prefetch=1,
            grid=(bsz // bb,),
            in_specs=[pl.BlockSpec((bb, t, N_QKV), lambda b, s: (b, 0, 0)),
                      pl.BlockSpec((bb, W, C_KV), lambda b, s: (b, 0, 0)),
                      pl.BlockSpec((bb, W, C_KV), lambda b, s: (b, 0, 0)),
                      pl.BlockSpec((1, C_ATTN), lambda b, s: (0, 0)),
                      pl.BlockSpec((1, C_KV), lambda b, s: (0, 0)),
                      pl.BlockSpec((LANES, LANES), lambda b, s: (0, 0))],
            out_specs=(pl.BlockSpec((bb, t, C_ATTN), lambda b, s: (b, 0, 0)),
                       pl.BlockSpec((bb, W, C_KV), lambda b, s: (b, 0, 0)),
                       pl.BlockSpec((bb, W, C_KV), lambda b, s: (b, 0, 0)))),
        compiler_params=_cparams("parallel"),
        name="swa_sample",
    )(sinks, qa3, ck, cv, qn, kn, ones128)


def _post_kernel(*refs, has_router):
    if has_router:
        (x_ref, y_ref, bonus_ref, g_ref, oa_ref, lnw_ref, lnb_ref, wt_ref, wb_ref, gf_ref, ones_ref,
         wrh_ref, wrl_ref, br_ref, xmid_ref, h_ref, route_ref) = refs
    else:
        (x_ref, y_ref, bonus_ref, g_ref, oa_ref, lnw_ref, lnb_ref, wt_ref, wb_ref, gf_ref, ones_ref,
         xmid_ref, h_ref) = refs
    y = y_ref[...]
    mu = _head_sum(y, ones_ref) * (1.0 / HEAD_DIM)
    d = y - mu
    var = _head_sum(d * d, ones_ref) * (1.0 / HEAD_DIM)
    yn = d * lax.rsqrt(var + LNX_EPS) * lnw_ref[...] + lnb_ref[...]
    o_r = (yn + bonus_ref[...]) * g_ref[...]
    mixed = _dot(o_r.astype(BF16), wt_ref[...]) + _dot(oa_ref[...].astype(BF16), wb_ref[...])
    x_mid = x_ref[...] + mixed
    xmid_ref[...] = x_mid
    h = _rms_rows(x_mid, gf_ref[...])
    h_ref[...] = h.astype(h_ref.dtype)
    if has_router:
        hh, hl = _split2(h)
        logits = _dot(hh, wrh_ref[...]) + _dot(hh, wrl_ref[...]) + _dot(hl, wrh_ref[...]) + br_ref[...]
        lane = lax.broadcasted_iota(jnp.int32, logits.shape, 1)
        lg = jnp.where(lane < N_EXPERTS, logits, -jnp.inf)
        m1 = jnp.max(lg, axis=-1, keepdims=True)
        i1 = jnp.min(jnp.where(lg == m1, lane, LANES), axis=-1, keepdims=True)
        lg2 = jnp.where(lane == i1, -jnp.inf, lg)
        m2 = jnp.max(lg2, axis=-1, keepdims=True)
        i2 = jnp.min(jnp.where(lg2 == m2, lane, LANES), axis=-1, keepdims=True)
        e = jnp.exp(m2 - m1)
        g1 = 1.0 / (1.0 + e)
        g2 = e * g1
        route_ref[...] = jnp.where(lane == 0, i1.astype(F32),
                                   jnp.where(lane == 1, i2.astype(F32),
                                             jnp.where(lane == 2, g1, jnp.where(lane == 3, g2, 0.0))))


def _post(x, y, bonus, g, o_a, lnw, lnb, w_top, w_bot, g_ffn, ones128, router, h_dtype):
    m, d = x.shape
    tm = _pick_tile(m, (256, 128, 64, 32, 16, 8))
    row = lambda w: pl.BlockSpec((tm, w), lambda i: (i, 0))
    ins = [x, y, bonus, g, o_a, lnw, lnb, w_top, w_bot, g_ffn, ones128]
    specs = [row(d)] + [row(C_RWKV)] * 4 + [_const_spec(a.shape) for a in ins[5:]]
    outs = [jax.ShapeDtypeStruct((m, d), F32), jax.ShapeDtypeStruct((m, d), h_dtype)]
    out_specs = [row(d), row(d)]
    if router is not None:
        ins += list(router)
        specs += [_const_spec(a.shape) for a in router]
        outs.append(jax.ShapeDtypeStruct((m, LANES), F32))
        out_specs.append(row(LANES))
    return pl.pallas_call(
        functools.partial(_post_kernel, has_router=router is not None),
        out_shape=tuple(outs),
        grid=(m // tm,),
        in_specs=specs,
        out_specs=tuple(out_specs),
        compiler_params=_cparams("parallel"),
        name="post_mix",
    )(*ins)


def _ffn_kernel(*refs, has_res):
    if has_res:
        be_ref, x_ref, wg_ref, wu_ref, wd_ref, res_ref, o_ref = refs
    else:
        be_ref, x_ref, wg_ref, wu_ref, wd_ref, o_ref = refs
    del be_ref
    x = x_ref[...].astype(BF16)
    d_ff = wg_ref.shape[2]
    acc = None
    for f in range(d_ff // FF_CHUNK):
        fs = slice(f * FF_CHUNK, (f + 1) * FF_CHUNK)
        gt = _dot(x, wg_ref[0, :, fs])
        up = _dot(x, wu_ref[0, :, fs])
        act = (gt * _sigmoid(gt) * up).astype(BF16)
        part = _dot(act, wd_ref[0, fs, :])
        acc = part if acc is None else acc + part
    if has_res:
        acc = acc + res_ref[...]
    o_ref[...] = acc


def _ffn(x, block_e, wg, wu, wd, res, tm):
    m, d = x.shape
    d_ff = wg.shape[2]
    row = pl.BlockSpec((tm, d), lambda i, be: (i, 0))
    ins = [x, wg, wu, wd]
    specs = [row,
             pl.BlockSpec((1, d, d_ff), lambda i, be: (be[i], 0, 0)),
             pl.BlockSpec((1, d, d_ff), lambda i, be: (be[i], 0, 0)),
             pl.BlockSpec((1, d_ff, d), lambda i, be: (be[i], 0, 0))]
    if res is not None:
        ins.append(res)
        specs.append(row)
    return pl.pallas_call(
        functools.partial(_ffn_kernel, has_res=res is not None),
        out_shape=jax.ShapeDtypeStruct((m, d), F32),
        grid_spec=pltpu.PrefetchScalarGridSpec(
            num_scalar_prefetch=1, grid=(m // tm,), in_specs=specs, out_specs=row),
        compiler_params=_cparams("arbitrary"),
        name="swiglu",
    )(block_e, *ins)


def _gather_kernel(idx_ref, src_ref, o_ref, sem, *, rows):
    base = pl.program_id(0) * rows

    def start(r, carry):
        pltpu.make_async_copy(src_ref.at[pl.ds(idx_ref[base + r], 1)], o_ref.at[pl.ds(r, 1)], sem).start()
        return carry

    def wait(r, carry):
        pltpu.make_async_copy(src_ref.at[pl.ds(0, 1)], o_ref.at[pl.ds(r, 1)], sem).wait()
        return carry

    lax.fori_loop(0, rows, start, 0)
    lax.fori_loop(0, rows, wait, 0)


def _gather_rows(src, idx, rows):
    n = idx.shape[0]
    d = src.shape[1]
    return pl.pallas_call(
        functools.partial(_gather_kernel, rows=rows),
        out_shape=jax.ShapeDtypeStruct((n, d), src.dtype),
        grid_spec=pltpu.PrefetchScalarGridSpec(
            num_scalar_prefetch=1, grid=(n // rows,),
            in_specs=[pl.BlockSpec(memory_space=pl.ANY)],
            out_specs=pl.BlockSpec((rows, d), lambda i, idx: (i, 0)),
            scratch_shapes=[pltpu.SemaphoreType.DMA(())]),
        compiler_params=_cparams("arbitrary"),
        name="gather_rows",
    )(idx, src)


def _combine_kernel(s0_ref, s1_ref, x_ref, route_ref, yb_ref, o_ref, buf, sem, *, rows):
    base = pl.program_id(0) * rows

    def start(r, carry):
        pltpu.make_async_copy(yb_ref.at[pl.ds(s0_ref[base + r], 1)], buf.at[0, pl.ds(r, 1)], sem.at[0]).start()
        pltpu.make_async_copy(yb_ref.at[pl.ds(s1_ref[base + r], 1)], buf.at[1, pl.ds(r, 1)], sem.at[1]).start()
        return carry

    def wait(r, carry):
        pltpu.make_async_copy(yb_ref.at[pl.ds(0, 1)], buf.at[0, pl.ds(r, 1)], sem.at[0]).wait()
        pltpu.make_async_copy(yb_ref.at[pl.ds(0, 1)], buf.at[1, pl.ds(r, 1)], sem.at[1]).wait()
        return carry

    lax.fori_loop(0, rows, start, 0)
    lax.fori_loop(0, rows, wait, 0)
    route = route_ref[...]
    o_ref[...] = x_ref[...] + route[:, 2:3] * buf[0] + route[:, 3:4] * buf[1]


def _moe_combine(x_mid, route, yb, slot0, slot1):
    m, d = x_mid.shape
    rows = _pick_tile(m, (256, 128, 64, 32, 16, 8))
    row = lambda w: pl.BlockSpec((rows, w), lambda i, a, b: (i, 0))
    return pl.pallas_call(
        functools.partial(_combine_kernel, rows=rows),
        out_shape=jax.ShapeDtypeStruct((m, d), F32),
        grid_spec=pltpu.PrefetchScalarGridSpec(
            num_scalar_prefetch=2, grid=(m // rows,),
            in_specs=[row(d), row(LANES), pl.BlockSpec(memory_space=pl.ANY)],
            out_specs=row(d),
            scratch_shapes=[pltpu.VMEM((2, rows, d), F32), pltpu.SemaphoreType.DMA((2,))]),
        compiler_params=_cparams("arbitrary"),
        name="moe_combine",
    )(slot0, slot1, x_mid, route, yb)


def _moe_plan(route, tm):
    m = route.shape[0]
    a_tot = m * TOP_K
    e_flat = route[:, :TOP_K].astype(jnp.int32).reshape(a_tot)
    tok_flat = jnp.arange(a_tot, dtype=jnp.int32) // TOP_K
    onehot = (e_flat[:, None] == jnp.arange(N_EXPERTS, dtype=jnp.int32)[None, :]).astype(jnp.int32)
    csum = jnp.cumsum(onehot, axis=0)
    rank = jnp.sum((csum - onehot) * onehot, axis=1)
    counts = csum[-1]
    padded = (counts + tm - 1) // tm * tm
    pad_end = jnp.cumsum(padded)
    pad_start = pad_end - padded
    slot = (jnp.sum(pad_start[None, :] * onehot, axis=1) + rank).astype(jnp.int32)
    n_blocks = -(-(a_tot + N_EXPERTS * (tm - 1)) // tm)
    slot_tok = jnp.zeros((n_blocks * tm,), jnp.int32).at[slot].set(tok_flat)
    blk_start = (jnp.arange(n_blocks, dtype=jnp.int32) * tm)[:, None]
    block_e = jnp.minimum(jnp.sum((pad_end[None, :] <= blk_start).astype(jnp.int32), axis=1), N_EXPERTS - 1)
    slot_of = slot.reshape(m, TOP_K)
    return slot_tok, block_e.astype(jnp.int32), slot_of[:, 0], slot_of[:, 1]


def _ones_block():
    i = jnp.arange(LANES) // HEAD_DIM
    return (i[:, None] == i[None, :]).astype(BF16)


def kernel(x_prompt, x_sample, state_wkv, state_shift, cache_swa_k, cache_swa_v, norm_mix, norm_ffn, w_in, mu_shift, w0, w2, a0, a2, g2, k_k, k_a, r_k, ln_x_w, ln_x_b, v0, v1, v2, q_norm, k_norm, sinks, w_out, w_gate_d, w_up_d, w_down_d, w_router, b_router, w_gate_e, w_up_e, w_down_e):
    depth = w_in.shape[0]
    bp, tp, d = x_prompt.shape
    bs, ts, _ = x_sample.shape
    mp, ms = bp * tp, bs * ts
    ones128 = _ones_block()
    x = jnp.concatenate([x_prompt.reshape(mp, d), x_sample.reshape(ms, d)], axis=0)
    row = lambda a: a.reshape(1, -1).astype(F32)

    shift_zero = jnp.zeros((bp, N_SHIFT), F32)
    wkv_zero = jnp.zeros((bp, H_RWKV, HEAD_DIM, HEAD_DIM), F32)
    vfirst_p = vfirst_s = None
    outs = {k_: [] for k_ in ("wkv_p", "wkv_s", "sh_p", "sh_s", "k_p", "v_p", "k_s", "v_s")}

    for l in range(depth):
        pr, qa = _norm_proj(x, norm_mix[l], w_in[l].astype(BF16))
        pr_p, pr_s = pr[:mp].reshape(bp, tp, N_SHIFT), pr[mp:].reshape(bs, ts, N_SHIFT)
        qa_p, qa_s = qa[:mp].reshape(bp, tp, N_QKV), qa[mp:].reshape(bs, ts, N_QKV)
        outs["sh_p"].append(pr_p[:, -1])
        outs["sh_s"].append(pr_s[:, -1])

        wwa = jnp.zeros((LANES, 2 * C_RWKV), F32)
        wwa = wwa.at[:D_DECAY, :C_RWKV].set(w2[l]).at[D_DECAY:, C_RWKV:].set(a2[l]).astype(BF16)
        prm = dict(mu=row(mu_shift[l]), w0=row(w0[l]), wwa=wwa, a0=row(a0[l]), g2=g2[l].astype(BF16),
                   k_k=row(k_k[l]), k_a=row(k_a[l]), r_k=row(r_k[l]))
        if l > 0:
            d_mv = v1.shape[2]
            prm["v0"] = row(v0[l - 1])
            prm["v1"] = jnp.zeros((C_RWKV, LANES), F32).at[:, :d_mv].set(v1[l - 1]).astype(BF16)
            prm["v2"] = jnp.zeros((LANES, C_RWKV), F32).at[:d_mv].set(v2[l - 1]).astype(BF16)
        r_p, lw_p, k_p, v_p, an_p, b_p, g_p, bonus_p = _rwkv_prep(pr_p, shift_zero, prm, vfirst_p, ones128)
        r_s, lw_s, k_s, v_s, an_s, b_s, g_s, bonus_s = _rwkv_prep(pr_s, state_shift[l], prm, vfirst_s, ones128)
        if l == 0:
            vfirst_p, vfirst_s = v_p, v_s

        y_p, wkv_p = _rwkv_scan(r_p, lw_p, k_p, v_p, an_p, b_p, wkv_zero)
        y_s, wkv_s = _rwkv_scan(r_s, lw_s, k_s, v_s, an_s, b_s, state_wkv[l])
        outs["wkv_p"].append(wkv_p)
        outs["wkv_s"].append(wkv_s)

        qn = jnp.tile(row(q_norm[l]), (1, H_ATTN))
        kn = jnp.tile(row(k_norm[l]), (1, KV_HEADS))
        sk = sinks[l].astype(F32)
        oa_p, kc_p = _swa_prompt(qa_p, sk, qn, kn, ones128)
        ck = cache_swa_k[l].reshape(bs, WINDOW, C_KV)
        cv = cache_swa_v[l].reshape(bs, WINDOW, C_KV)
        oa_s, nk_s, nv_s = _swa_sample(qa_s, ck, cv, sk, qn, kn, ones128)
        outs["k_p"].append(kc_p.reshape(bp, WINDOW, KV_HEADS, HEAD_DIM))
        outs["v_p"].append(qa_p[:, tp - WINDOW:, C_ATTN + C_KV:].reshape(bp, WINDOW, KV_HEADS, HEAD_DIM))
        outs["k_s"].append(nk_s.reshape(bs, WINDOW, KV_HEADS, HEAD_DIM))
        outs["v_s"].append(nv_s.reshape(bs, WINDOW, KV_HEADS, HEAD_DIM))

        cat = lambda a, b: jnp.concatenate([a.reshape(mp, -1), b.reshape(ms, -1)], axis=0)
        wo = w_out[l].astype(BF16)
        moe = l % 2 == 1
        j = l // 2
        router = None
        if moe:
            wr = jnp.zeros((d, LANES), F32).at[:, :N_EXPERTS].set(w_router[j])
            wr_hi = wr.astype(BF16)
            wr_lo = (wr - wr_hi.astype(F32)).astype(BF16)
            br = jnp.zeros((1, LANES), F32).at[0, :N_EXPERTS].set(b_router[j])
            router = (wr_hi, wr_lo, br)
        post = _post(x, cat(y_p, y_s), cat(bonus_p, bonus_s), cat(g_p, g_s), cat(oa_p, oa_s),
                     row(ln_x_w[l]), row(ln_x_b[l]), wo[:C_RWKV], wo[C_RWKV:], row(norm_ffn[l]), ones128,
                     router, F32 if moe else BF16)
        if not moe:
            x_mid, h = post
            tm = _pick_tile(mp + ms, (512, 256, 128, 64, 32, 16, 8))
            x = _ffn(h, jnp.zeros(((mp + ms) // tm,), jnp.int32), w_gate_d[j][None].astype(BF16),
                     w_up_d[j][None].astype(BF16), w_down_d[j][None].astype(BF16), x_mid, tm)
        else:
            x_mid, h, route = post
            tm = 256 if (mp + ms) >= 4096 else 16
            slot_tok, block_e, slot0, slot1 = _moe_plan(route, tm)
            xs = _gather_rows(h, slot_tok, tm)
            yb = _ffn(xs, block_e, w_gate_e[j].astype(BF16), w_up_e[j].astype(BF16), w_down_e[j].astype(BF16),
                      None, tm)
            x = _moe_combine(x_mid, route, yb, slot0, slot1)

    st = lambda name: jnp.stack(outs[name])
    return (x[:mp].reshape(bp, tp, d), x[mp:].reshape(bs, ts, d), st("wkv_p"), st("wkv_s"), st("sh_p"), st("sh_s"),
            st("k_p"), st("v_p"), st("k_s"), st("v_s"))
```

```python
import functools

import jax
import jax.numpy as jnp
from jax import lax
from jax.experimental import pallas as pl
from jax.experimental.pallas import tpu as pltpu

F32 = jnp.float32
BF16 = jnp.bfloat16

HEAD_DIM = 64
C_RWKV = 512
H_RWKV = C_RWKV // HEAD_DIM
C_ATTN = 512
H_ATTN = C_ATTN // HEAD_DIM
KV_HEADS = 2
GQA_GROUP = H_ATTN // KV_HEADS
C_KV = KV_HEADS * HEAD_DIM
WINDOW = 128
ATTN_BLOCK = 128
D_DECAY = 64
D_AAA = 64
D_GATE = 128
N_SHIFT = 3 * C_RWKV + D_DECAY + D_AAA + D_GATE
N_QKV = C_ATTN + 2 * C_KV
N_EXPERTS = 8
TOP_K = 2
RMS_EPS = 1e-6
LNX_EPS = 64e-5
NEG = -1e30
LANES = 128
FF_CHUNK = 256
VMEM_LIMIT = 56 * 1024 * 1024


def _cparams(*sem):
    return pltpu.CompilerParams(dimension_semantics=sem, vmem_limit_bytes=VMEM_LIMIT)


def _pick_tile(n, prefs):
    for t in prefs:
        if n % t == 0:
            return t
    return n


def _const_spec(shape):
    nd = len(shape)
    return pl.BlockSpec(shape, lambda *_: (0,) * nd)


def _dual_specs(tm, width, n_first):
    return [pl.BlockSpec((tm, width), lambda i, *_: (jnp.minimum(i, n_first - 1), 0)),
            pl.BlockSpec((tm, width), lambda i, *_: (jnp.maximum(i - n_first, 0), 0))]


def _pick_rows(a_ref, b_ref, n_first):
    return jnp.where(pl.program_id(0) < n_first, a_ref[...], b_ref[...])


def _split2(x):
    hi = x.astype(BF16)
    lo = (x - hi.astype(F32)).astype(BF16)
    return hi, lo


def _dot(a, b):
    return jnp.dot(a, b, preferred_element_type=F32)


def _dot_nt(a, b):
    return lax.dot_general(a, b, (((1,), (1,)), ((), ())), preferred_element_type=F32)


def _dot_tn(a, b):
    return lax.dot_general(a, b, (((0,), (0,)), ((), ())), preferred_element_type=F32)


def _bf(x):
    return x.astype(BF16)


def _head_sum(x, ones_ref):
    ones = ones_ref[...]
    outs = []
    for c in range(x.shape[1] // LANES):
        hi, lo = _split2(x[:, c * LANES:(c + 1) * LANES])
        outs.append(_dot(hi, ones) + _dot(lo, ones))
    return outs[0] if len(outs) == 1 else jnp.concatenate(outs, axis=1)


def _sigmoid(x):
    return 1.0 / (1.0 + jnp.exp(-x))


def _rms_rows(x, g):
    return x * lax.rsqrt(jnp.mean(x * x, axis=-1, keepdims=True) + RMS_EPS) * g


def _norm_proj_kernel(*refs, n_first):
    if n_first is None:
        x_ref, g_ref, w_ref, pr_ref, qa_ref = refs
        x = x_ref[...]
    else:
        xa_ref, xb_ref, g_ref, w_ref, pr_ref, qa_ref = refs
        x = _pick_rows(xa_ref, xb_ref, n_first)
    h = _rms_rows(x, g_ref[...])
    y = _dot(_bf(h), w_ref[...])
    pr_ref[...] = y[:, :N_SHIFT]
    qa_ref[...] = y[:, N_SHIFT:]


def _norm_proj(xs, g, w_bf16):
    m = sum(a.shape[0] for a in xs)
    d = xs[0].shape[1]
    n_in = w_bf16.shape[1]
    tm = _pick_tile(xs[-1].shape[0] if len(xs) == 2 else m, (512, 256, 128, 64, 32, 16, 8))
    if len(xs) == 2:
        n_first = xs[0].shape[0] // tm
        x_specs = _dual_specs(tm, d, n_first)
    else:
        n_first = None
        x_specs = [pl.BlockSpec((tm, d), lambda i: (i, 0))]
    return pl.pallas_call(
        functools.partial(_norm_proj_kernel, n_first=n_first),
        out_shape=(jax.ShapeDtypeStruct((m, N_SHIFT), F32), jax.ShapeDtypeStruct((m, N_QKV), F32)),
        grid=(m // tm,),
        in_specs=x_specs + [_const_spec((1, d)), _const_spec((d, n_in))],
        out_specs=(pl.BlockSpec((tm, N_SHIFT), lambda i: (i, 0)), pl.BlockSpec((tm, N_QKV), lambda i: (i, 0))),
        compiler_params=_cparams("parallel"),
        name="norm_proj",
    )(*xs, g.reshape(1, d), w_bf16)


def _rwkv_prep_kernel(*refs, has_vfirst, bs, tt):
    if has_vfirst:
        (pr_ref, shift_ref, mu_ref, w0_ref, wwa_ref, a0_ref, g2_ref, kk_ref, ka_ref, rk_ref, ones_ref,
         vf_ref, v0_ref, v1_ref, v2_ref,
         r_out, lw_out, k_out, v_out, an_out, b_out, g_out, bonus_out, carry_ref) = refs
    else:
        (pr_ref, shift_ref, mu_ref, w0_ref, wwa_ref, a0_ref, g2_ref, kk_ref, ka_ref, rk_ref, ones_ref,
         r_out, lw_out, k_out, v_out, an_out, b_out, g_out, bonus_out, carry_ref) = refs
    rows, n = pr_ref.shape
    j = pl.program_id(1)

    x = pr_ref[...]
    first3 = jnp.where(j == 0, shift_ref[...], carry_ref[...])
    carry_ref[...] = x.reshape(bs, tt, n)[:, tt - 1:tt, :]
    first = jnp.broadcast_to(first3, (bs, tt, n)).reshape(rows, n)
    rolled = pltpu.roll(x, 1, 0)
    tpos = lax.broadcasted_iota(jnp.int32, (rows, n), 0) % tt
    prev = jnp.where(tpos == 0, first, rolled)
    prm = x + (prev - x) * mu_ref[...]

    r = prm[:, 0:C_RWKV]
    k = prm[:, C_RWKV:2 * C_RWKV]
    v = prm[:, 2 * C_RWKV:3 * C_RWKV]
    wa = prm[:, 3 * C_RWKV:3 * C_RWKV + LANES]
    gl = prm[:, 3 * C_RWKV + LANES:3 * C_RWKV + 2 * LANES]

    lane = lax.broadcasted_iota(jnp.int32, wa.shape, 1)
    wa_in = _bf(jnp.where(lane < D_DECAY, jnp.tanh(wa), wa))
    wa_out = _dot(wa_in, wwa_ref[...])
    z = w0_ref[...] + wa_out[:, :C_RWKV]
    softplus = jnp.maximum(-z, 0.0) + jnp.log(1.0 + jnp.exp(-jnp.abs(z)))
    lw = -jnp.exp(-softplus - 0.5)
    a = _sigmoid(a0_ref[...] + wa_out[:, C_RWKV:])
    g = _dot(_bf(_sigmoid(gl)), g2_ref[...])

    if has_vfirst:
        t1 = _dot(_bf(v), v1_ref[...])
        t2 = _dot(_bf(t1), v2_ref[...])
        v = v + (vf_ref[...] - v) * _sigmoid(v0_ref[...] + t2)

    kk = k * kk_ref[...]
    norm = jnp.maximum(jnp.sqrt(_head_sum(kk * kk, ones_ref)), 1e-12)
    kk = kk / norm
    k = k * (1.0 + (a - 1.0) * ka_ref[...])
    bonus = _head_sum(r * k * rk_ref[...], ones_ref) * v

    r_out[...] = r
    lw_out[...] = lw
    k_out[...] = k
    v_out[...] = v
    an_out[...] = -kk
    b_out[...] = kk * a
    g_out[...] = g
    bonus_out[...] = bonus


def _rwkv_prep(pr, row0, nseq, t, shift, p, vfirst, ones128):
    n = pr.shape[1]
    if t >= 64:
        bs, tt = 1, _pick_tile(t, (256, 128, 64))
    else:
        bs, tt = _pick_tile(nseq, (16, 8, 4, 2, 1)), t
    rows = bs * tt
    nt = t // tt
    blk0 = row0 // rows
    ins = [pr, shift.reshape(nseq, 1, n), p["mu"], p["w0"], p["wwa"], p["a0"], p["g2"], p["k_k"], p["k_a"],
           p["r_k"], ones128]
    out_spec = pl.BlockSpec((rows, C_RWKV), lambda i, j: (i * nt + j, 0))
    specs = [pl.BlockSpec((rows, n), lambda i, j: (blk0 + i * nt + j, 0)),
             pl.BlockSpec((bs, 1, n), lambda i, j: (i, 0, 0))] + [_const_spec(a.shape) for a in ins[2:]]
    has_vfirst = vfirst is not None
    if has_vfirst:
        extra = [vfirst, p["v0"], p["v1"], p["v2"]]
        ins += extra
        specs += [out_spec] + [_const_spec(a.shape) for a in extra[1:]]
    out_sds = jax.ShapeDtypeStruct((nseq * t, C_RWKV), F32)
    return pl.pallas_call(
        functools.partial(_rwkv_prep_kernel, has_vfirst=has_vfirst, bs=bs, tt=tt),
        out_shape=(out_sds,) * 8,
        grid=(nseq // bs, nt),
        in_specs=specs,
        out_specs=(out_spec,) * 8,
        scratch_shapes=[pltpu.VMEM((bs, 1, n), F32)],
        compiler_params=_cparams("parallel", "arbitrary"),
        name="rwkv_prep",
    )(*ins)


def _scan_chunk(r, lw, k, v, an, b, s_list, chunk):
    C = chunk
    H = range(H_RWKV)
    row = lax.broadcasted_iota(jnp.int32, (C, C), 0)
    col = lax.broadcasted_iota(jnp.int32, (C, C), 1)
    incl = row >= col
    strict = row > col
    tri = _bf(jnp.where(incl, 1.0, 0.0))
    l1 = _bf(lw)
    rem = lw - l1.astype(F32)
    l2 = _bf(rem)
    l3 = _bf(rem - l2.astype(F32))
    cum = _dot(tri, l1) + _dot(tri, l2) + _dot(tri, l3)
    cum_last = cum[C - 1:C, :]
    e_neg = jnp.exp(-cum)
    e_d = jnp.exp(cum_last - cum)
    pc = jnp.exp(cum_last)
    eye = jnp.where(row == col, 1.0, 0.0)
    hs = lambda x, h: x[:, h * HEAD_DIM:(h + 1) * HEAD_DIM]

    lhs = _bf(jnp.concatenate([an * jnp.exp(cum - lw), r * jnp.exp(cum)], axis=0))
    bq, kq = _bf(b * e_neg), _bf(k * e_neg)
    bd, kd = _bf(b * e_d), _bf(k * e_d)
    vb = _bf(v)
    m_b = [_dot_nt(hs(lhs, h), hs(bq, h)) for h in H]
    m_k = [_dot_nt(hs(lhs, h), hs(kq, h)) for h in H]
    a_ab = [jnp.where(strict, m[:C], 0.0) for m in m_b]
    a_rb = [_bf(jnp.where(incl, m[C:], 0.0)) for m in m_b]
    row2 = lax.broadcasted_iota(jnp.int32, (2 * C, C), 0)
    col2 = lax.broadcasted_iota(jnp.int32, (2 * C, C), 1)
    mask2 = jnp.where(row2 < C, row2, row2 - C + 1) > col2
    a_k = [_bf(jnp.where(mask2, m, 0.0)) for m in m_k]
    kv = [_dot(a_k[h], hs(vb, h)) for h in H]
    skv = [_dot_tn(hs(vb, h), hs(kd, h)) for h in H]

    t_inv = [eye + a for a in a_ab]
    n_sq = max(C.bit_length() - 2, 0)
    if n_sq:
        a_b = [_bf(a) for a in a_ab]
        p_pow = [_dot(a_b[h], a_b[h]) for h in H]
        for lev in range(n_sq):
            p_b = [_bf(p) for p in p_pow]
            t_inv = [t_inv[h] + _dot(_bf(t_inv[h]), p_b[h]) for h in H]
            if lev + 1 < n_sq:
                p_pow = [_dot(p_b[h], p_b[h]) for h in H]
    t_b = [_bf(t) for t in t_inv]
    w1 = [_dot(t_b[h], hs(lhs[:C], h)) for h in H]
    u0 = [_dot(t_b[h], _bf(kv[h][:C])) for h in H]
    s_b = [_bf(s) for s in s_list]
    x1 = [_dot_nt(jnp.concatenate([_bf(w1[h]), hs(lhs[C:], h)], axis=0), s_b[h]) for h in H]
    u_b = [_bf(x1[h][:C] + u0[h]) for h in H]
    y = [x1[h][C:] + _dot(a_rb[h], u_b[h]) + kv[h][C:] for h in H]
    s_new = [s_list[h] * hs(pc, h) + _dot_tn(u_b[h], hs(bd, h)) + skv[h] for h in H]
    return jnp.concatenate(y, axis=1), s_new


def _scan_kernel(r_ref, lw_ref, k_ref, v_ref, an_ref, b_ref, s0_ref, y_ref, sout_ref, s_scr, *, chunk, nb):
    c = pl.program_id(1)

    @pl.when(c == 0)
    def _():
        s_scr[...] = s0_ref[...]

    def body(i, carry):
        rows = pl.ds(0, chunk) if nb == 1 else pl.ds(pl.multiple_of(i * chunk, chunk), chunk)
        s_list = [s_scr[i, h] for h in range(H_RWKV)]
        y, s_new = _scan_chunk(r_ref[rows, :], lw_ref[rows, :], k_ref[rows, :], v_ref[rows, :], an_ref[rows, :],
                               b_ref[rows, :], s_list, chunk)
        y_ref[rows, :] = y
        for h in range(H_RWKV):
            s_scr[i, h] = s_new[h]
        return carry

    if nb == 1:
        body(0, 0)
    else:
        lax.fori_loop(0, nb, body, 0)

    @pl.when(c == pl.num_programs(1) - 1)
    def _():
        sout_ref[...] = s_scr[...]


def _rwkv_scan(r, lw, k, v, an, b, s0, t):
    nseq = s0.shape[0]
    chunk = _pick_tile(t, (64, 32, 16, 8))
    nb = 1 if t > chunk else _pick_tile(nseq, (8, 4, 2, 1))
    nc = t // chunk
    seq_spec = pl.BlockSpec((nb * chunk, C_RWKV), lambda i, j: (i * nc + j, 0))
    st_spec = pl.BlockSpec((nb, H_RWKV, HEAD_DIM, HEAD_DIM), lambda i, j: (i, 0, 0, 0))
    return pl.pallas_call(
        functools.partial(_scan_kernel, chunk=chunk, nb=nb),
        out_shape=(jax.ShapeDtypeStruct((nseq * t, C_RWKV), F32),
                   jax.ShapeDtypeStruct((nseq, H_RWKV, HEAD_DIM, HEAD_DIM), F32)),
        grid=(nseq // nb, nc),
        in_specs=[seq_spec] * 6 + [st_spec],
        out_specs=(seq_spec, st_spec),
        scratch_shapes=[pltpu.VMEM((nb, H_RWKV, HEAD_DIM, HEAD_DIM), F32)],
        compiler_params=_cparams("parallel", "arbitrary"),
        name="rwkv_scan",
    )(r, lw, k, v, an, b, s0)


def _qk_norm(x, g, ones_ref):
    ms = _head_sum(x * x, ones_ref) * (1.0 / HEAD_DIM)
    return x * lax.rsqrt(ms + RMS_EPS) * g


def _sink_softmax_rows(s_parts, sink):
    m = sink
    for s in s_parts:
        m = jnp.maximum(m, jnp.max(s, axis=-1, keepdims=True))
    ps = [jnp.exp(s - m) for s in s_parts]
    den = jnp.exp(sink - m)
    for p in ps:
        den = den + jnp.sum(p, axis=-1, keepdims=True)
    inv = 1.0 / den
    return [p * inv for p in ps]


def _group_rows(q, g, t):
    return jnp.concatenate(
        [q[:, (g * GQA_GROUP + a) * HEAD_DIM:(g * GQA_GROUP + a + 1) * HEAD_DIM] for a in range(GQA_GROUP)], axis=0)


def _group_sinks(sink_ref, g, rows, t):
    rowh = lax.broadcasted_iota(jnp.int32, (rows, 1), 0) // t
    sink = jnp.zeros((rows, 1), F32)
    for a in range(GQA_GROUP):
        sink = jnp.where(rowh == a, sink_ref[g * GQA_GROUP + a], sink)
    return sink


def _swa_prompt_kernel(sink_ref, cur_ref, prev_ref, qn_ref, kn_ref, ones_ref, o_ref, kc_ref):
    n = pl.program_id(1)
    L = ATTN_BLOCK
    rows = GQA_GROUP * L
    cur = cur_ref[...]
    prev = prev_ref[...]
    q = _bf(_qk_norm(cur[:, :C_ATTN], qn_ref[...], ones_ref))
    k_cur = _qk_norm(cur[:, C_ATTN:C_ATTN + C_KV], kn_ref[...], ones_ref)
    k_prev = _qk_norm(prev[:, C_ATTN:C_ATTN + C_KV], kn_ref[...], ones_ref)
    kc_ref[0] = k_cur
    kk = _bf(jnp.concatenate([k_prev, k_cur], axis=0))
    vv = _bf(jnp.concatenate([prev[:, C_ATTN + C_KV:], cur[:, C_ATTN + C_KV:]], axis=0))
    qq = lax.broadcasted_iota(jnp.int32, (rows, 2 * L), 0) % L
    jj = lax.broadcasted_iota(jnp.int32, (rows, 2 * L), 1)
    mask = (jj > qq) & (jj <= qq + L) & ((jj >= L) | (n > 0))
    G = range(KV_HEADS)
    ksl = lambda x, g: x[:, g * HEAD_DIM:(g + 1) * HEAD_DIM]
    s = [jnp.where(mask, _dot_nt(_group_rows(q, g, L), ksl(kk, g)) * (HEAD_DIM ** -0.5), NEG) for g in G]
    p = [_sink_softmax_rows([s[g]], _group_sinks(sink_ref, g, rows, L))[0] for g in G]
    o = [_dot(_bf(p[g]), ksl(vv, g)) for g in G]
    o_ref[...] = jnp.concatenate([o[g][a * L:(a + 1) * L, :] for g in G for a in range(GQA_GROUP)], axis=1)


def _swa_prompt(qa, bsz, t, sinks, qn, kn, ones128):
    L = ATTN_BLOCK
    nb = t // L
    return pl.pallas_call(
        _swa_prompt_kernel,
        out_shape=(jax.ShapeDtypeStruct((bsz * t, C_ATTN), F32), jax.ShapeDtypeStruct((bsz, L, C_KV), F32)),
        grid_spec=pltpu.PrefetchScalarGridSpec(
            num_scalar_prefetch=1,
            grid=(bsz, nb),
            in_specs=[pl.BlockSpec((L, N_QKV), lambda b, n, s: (b * nb + n, 0)),
                      pl.BlockSpec((L, N_QKV), lambda b, n, s: (b * nb + jnp.maximum(n - 1, 0), 0)),
                      pl.BlockSpec((1, C_ATTN), lambda b, n, s: (0, 0)),
                      pl.BlockSpec((1, C_KV), lambda b, n, s: (0, 0)),
                      pl.BlockSpec((LANES, LANES), lambda b, n, s: (0, 0))],
            out_specs=(pl.BlockSpec((L, C_ATTN), lambda b, n, s: (b * nb + n, 0)),
                       pl.BlockSpec((1, L, C_KV), lambda b, n, s: (b, 0, 0)))),
        compiler_params=_cparams("parallel", "arbitrary"),
        name="swa_prompt",
    )(sinks, qa, qa, qn, kn, ones128)


def _swa_sample_kernel(sink_ref, qa_ref, ck_ref, cv_ref, qn_ref, kn_ref, ones_ref, o_ref, nk_ref, nv_ref, *, bb, t):
    W = WINDOW
    rows = GQA_GROUP * t
    rr = lax.broadcasted_iota(jnp.int32, (rows, W), 0) % t
    jc = lax.broadcasted_iota(jnp.int32, (rows, W), 1)
    mask_c = jc > rr
    rn = lax.broadcasted_iota(jnp.int32, (rows, t), 0) % t
    jn = lax.broadcasted_iota(jnp.int32, (rows, t), 1)
    mask_n = jn <= rn
    G = range(KV_HEADS)
    ksl = lambda x, g: x[:, g * HEAD_DIM:(g + 1) * HEAD_DIM]
    scale = HEAD_DIM ** -0.5

    def body(i, carry):
        x = qa_ref[pl.ds(pl.multiple_of(i * t, t), t), :]
        q = _bf(_qk_norm(x[:, :C_ATTN], qn_ref[...], ones_ref))
        k_new = _qk_norm(x[:, C_ATTN:C_ATTN + C_KV], kn_ref[...], ones_ref)
        v_new = x[:, C_ATTN + C_KV:]
        ck = ck_ref[i]
        cv = cv_ref[i]
        nk_ref[i, 0:W - t, :] = ck[t:, :]
        nk_ref[i, W - t:W, :] = k_new
        nv_ref[i, 0:W - t, :] = cv[t:, :]
        nv_ref[i, W - t:W, :] = v_new
        ckb, cvb, knb, vnb = _bf(ck), _bf(cv), _bf(k_new), _bf(v_new)
        qg = [_group_rows(q, g, t) for g in G]
        s_c = [jnp.where(mask_c, _dot_nt(qg[g], ksl(ckb, g)) * scale, NEG) for g in G]
        s_n = [jnp.where(mask_n, _dot_nt(qg[g], ksl(knb, g)) * scale, NEG) for g in G]
        p = [_sink_softmax_rows([s_c[g], s_n[g]], _group_sinks(sink_ref, g, rows, t)) for g in G]
        og = [_dot(_bf(p[g][0]), ksl(cvb, g)) + _dot(_bf(p[g][1]), ksl(vnb, g)) for g in G]
        o_ref[pl.ds(pl.multiple_of(i * t, t), t), :] = jnp.concatenate(
            [og[g][a * t:(a + 1) * t, :] for g in G for a in range(GQA_GROUP)], axis=1)
        return carry

    lax.fori_loop(0, bb, body, 0)


def _swa_sample(qa, row0, bsz, t, ck, cv, sinks, qn, kn, ones128):
    W = WINDOW
    bb = _pick_tile(bsz, (16, 8, 4, 2, 1))
    blk0 = row0 // (bb * t)
    sds = jax.ShapeDtypeStruct
    return pl.pallas_call(
        functools.partial(_swa_sample_kernel, bb=bb, t=t),
        out_shape=(sds((bsz * t, C_ATTN), F32), sds((bsz, W, C_KV), F32), sds((bsz, W, C_KV), F32)),
        grid_spec=pltpu.PrefetchScalarGridSpec(
            num_scalar_prefetch=1,
            grid=(bsz // bb,),
            in_specs=[pl.BlockSpec((bb * t, N_QKV), lambda b, s: (blk0 + b, 0)),
                      pl.BlockSpec((bb, W, C_KV), lambda b, s: (b, 0, 0)),
                      pl.BlockSpec((bb, W, C_KV), lambda b, s: (b, 0, 0)),
                      pl.BlockSpec((1, C_ATTN), lambda b, s: (0, 0)),
                      pl.BlockSpec((1, C_KV), lambda b, s: (0, 0)),
                      pl.BlockSpec((LANES, LANES), lambda b, s: (0, 0))],
            out_specs=(pl.BlockSpec((bb * t, C_ATTN), lambda b, s: (b, 0)),
                       pl.BlockSpec((bb, W, C_KV), lambda b, s: (b, 0, 0)),
                       pl.BlockSpec((bb, W, C_KV), lambda b, s: (b, 0, 0)))),
        compiler_params=_cparams("parallel"),
        name="swa_sample",
    )(sinks, qa, ck, cv, qn, kn, ones128)


def _post_kernel(*refs, has_router, n_first, n_x):
    xs, refs = refs[:n_x], refs[n_x:]
    pairs, refs = refs[:8], refs[8:]
    if has_router:
        (lnw_ref, lnb_ref, wt_ref, wb_ref, gf_ref, ones_ref, wrh_ref, wrl_ref, br_ref,
         xmid_ref, h_ref, route_ref) = refs
    else:
        lnw_ref, lnb_ref, wt_ref, wb_ref, gf_ref, ones_ref, xmid_ref, h_ref = refs
    x = xs[0][...] if n_x == 1 else _pick_rows(xs[0], xs[1], n_first)
    y, bonus, g, o_a = [_pick_rows(pairs[2 * i], pairs[2 * i + 1], n_first) for i in range(4)]
    mu = _head_sum(y, ones_ref) * (1.0 / HEAD_DIM)
    d = y - mu
    var = _head_sum(d * d, ones_ref) * (1.0 / HEAD_DIM)
    yn = d * lax.rsqrt(var + LNX_EPS) * lnw_ref[...] + lnb_ref[...]
    o_r = (yn + bonus) * g
    mixed = _dot(_bf(o_r), wt_ref[...]) + _dot(_bf(o_a), wb_ref[...])
    x_mid = x + mixed
    xmid_ref[...] = x_mid
    h = _rms_rows(x_mid, gf_ref[...])
    h_ref[...] = h.astype(h_ref.dtype)
    if has_router:
        hh, hl = _split2(h)
        logits = _dot(hh, wrh_ref[...]) + _dot(hh, wrl_ref[...]) + _dot(hl, wrh_ref[...]) + br_ref[...]
        lane = lax.broadcasted_iota(jnp.int32, logits.shape, 1)
        lg = jnp.where(lane < N_EXPERTS, logits, -jnp.inf)
        m1 = jnp.max(lg, axis=-1, keepdims=True)
        i1 = jnp.min(jnp.where(lg == m1, lane, LANES), axis=-1, keepdims=True)
        lg2 = jnp.where(lane == i1, -jnp.inf, lg)
        m2 = jnp.max(lg2, axis=-1, keepdims=True)
        i2 = jnp.min(jnp.where(lg2 == m2, lane, LANES), axis=-1, keepdims=True)
        e = jnp.exp(m2 - m1)
        g1 = 1.0 / (1.0 + e)
        g2 = e * g1
        route_ref[...] = jnp.where(lane == 0, i1.astype(F32),
                                   jnp.where(lane == 1, i2.astype(F32),
                                             jnp.where(lane == 2, g1, jnp.where(lane == 3, g2, 0.0))))


def _post(xs, pairs, lnw, lnb, w_top, w_bot, g_ffn, ones128, router, h_dtype):
    d = xs[0].shape[1]
    m = sum(a.shape[0] for a in pairs[0])
    tm = _pick_tile(pairs[0][1].shape[0], (256, 128, 64, 32, 16, 8))
    n_first = pairs[0][0].shape[0] // tm
    row = lambda w: pl.BlockSpec((tm, w), lambda i: (i, 0))
    consts = [lnw, lnb, w_top, w_bot, g_ffn, ones128] + (list(router) if router is not None else [])
    ins = list(xs) + [a for pr_ in pairs for a in pr_] + consts
    specs = (_dual_specs(tm, d, n_first) if len(xs) == 2 else [row(d)])
    for _ in pairs:
        specs += _dual_specs(tm, C_RWKV, n_first)
    specs += [_const_spec(a.shape) for a in consts]
    outs = [jax.ShapeDtypeStruct((m, d), F32), jax.ShapeDtypeStruct((m, d), h_dtype)]
    out_specs = [row(d), row(d)]
    if router is not None:
        outs.append(jax.ShapeDtypeStruct((m, LANES), F32))
        out_specs.append(row(LANES))
    return pl.pallas_call(
        functools.partial(_post_kernel, has_router=router is not None, n_first=n_first, n_x=len(xs)),
        out_shape=tuple(outs),
        grid=(m // tm,),
        in_specs=specs,
        out_specs=tuple(out_specs),
        compiler_params=_cparams("parallel"),
        name="post_mix",
    )(*ins)


def _ffn_kernel(*refs, has_res):
    if has_res:
        be_ref, x_ref, wg_ref, wu_ref, wd_ref, res_ref, o_ref = refs
    else:
        be_ref, x_ref, wg_ref, wu_ref, wd_ref, o_ref = refs
    del be_ref
    x = _bf(x_ref[...])
    d_ff = wg_ref.shape[2]
    acc = None
    for f in range(d_ff // FF_CHUNK):
        fs = slice(f * FF_CHUNK, (f + 1) * FF_CHUNK)
        gt = _dot(x, wg_ref[0, :, fs])
        up = _dot(x, wu_ref[0, :, fs])
        act = _bf(gt * _sigmoid(gt) * up)
        part = _dot(act, wd_ref[0, fs, :])
        acc = part if acc is None else acc + part
    if has_res:
        acc = acc + res_ref[...]
    o_ref[...] = acc


def _ffn(x, block_e, wg, wu, wd, res, tm):
    m, d = x.shape
    d_ff = wg.shape[2]
    row = pl.BlockSpec((tm, d), lambda i, be: (i, 0))
    ins = [x, wg, wu, wd]
    specs = [row,
             pl.BlockSpec((1, d, d_ff), lambda i, be: (be[i], 0, 0)),
             pl.BlockSpec((1, d, d_ff), lambda i, be: (be[i], 0, 0)),
             pl.BlockSpec((1, d_ff, d), lambda i, be: (be[i], 0, 0))]
    if res is not None:
        ins.append(res)
        specs.append(row)
    return pl.pallas_call(
        functools.partial(_ffn_kernel, has_res=res is not None),
        out_shape=jax.ShapeDtypeStruct((m, d), F32),
        grid_spec=pltpu.PrefetchScalarGridSpec(
            num_scalar_prefetch=1, grid=(m // tm,), in_specs=specs, out_specs=row),
        compiler_params=_cparams("arbitrary"),
        name="swiglu",
    )(block_e, *ins)


def _gather_kernel(idx_ref, src_ref, o_ref, sem, *, rows):
    base = pl.program_id(0) * rows

    def start(r, carry):
        pltpu.make_async_copy(src_ref.at[pl.ds(idx_ref[base + r], 1)], o_ref.at[pl.ds(r, 1)], sem).start()
        return carry

    def wait(r, carry):
        pltpu.make_async_copy(src_ref.at[pl.ds(0, 1)], o_ref.at[pl.ds(r, 1)], sem).wait()
        return carry

    lax.fori_loop(0, rows, start, 0)
    lax.fori_loop(0, rows, wait, 0)


def _gather_rows(src, idx, rows):
    n = idx.shape[0]
    d = src.shape[1]
    return pl.pallas_call(
        functools.partial(_gather_kernel, rows=rows),
        out_shape=jax.ShapeDtypeStruct((n, d), src.dtype),
        grid_spec=pltpu.PrefetchScalarGridSpec(
            num_scalar_prefetch=1, grid=(n // rows,),
            in_specs=[pl.BlockSpec(memory_space=pl.ANY)],
            out_specs=pl.BlockSpec((rows, d), lambda i, idx: (i, 0)),
            scratch_shapes=[pltpu.SemaphoreType.DMA(())]),
        compiler_params=_cparams("arbitrary"),
        name="gather_rows",
    )(idx, src)


def _combine_kernel(s0_ref, s1_ref, x_ref, route_ref, yb_ref, o_ref, buf, sem, *, rows):
    base = pl.program_id(0) * rows

    def start(r, carry):
        pltpu.make_async_copy(yb_ref.at[pl.ds(s0_ref[base + r], 1)], buf.at[0, pl.ds(r, 1)], sem.at[0]).start()
        pltpu.make_async_copy(yb_ref.at[pl.ds(s1_ref[base + r], 1)], buf.at[1, pl.ds(r, 1)], sem.at[1]).start()
        return carry

    def wait(r, carry):
        pltpu.make_async_copy(yb_ref.at[pl.ds(0, 1)], buf.at[0, pl.ds(r, 1)], sem.at[0]).wait()
        pltpu.make_async_copy(yb_ref.at[pl.ds(0, 1)], buf.at[1, pl.ds(r, 1)], sem.at[1]).wait()
        return carry

    lax.fori_loop(0, rows, start, 0)
    lax.fori_loop(0, rows, wait, 0)
    route = route_ref[...]
    o_ref[...] = x_ref[...] + route[:, 2:3] * buf[0] + route[:, 3:4] * buf[1]


def _moe_combine(x_mid, route, yb, slot0, slot1):
    m, d = x_mid.shape
    rows = _pick_tile(m, (256, 128, 64, 32, 16, 8))
    row = lambda w: pl.BlockSpec((rows, w), lambda i, a, b: (i, 0))
    return pl.pallas_call(
        functools.partial(_combine_kernel, rows=rows),
        out_shape=jax.ShapeDtypeStruct((m, d), F32),
        grid_spec=pltpu.PrefetchScalarGridSpec(
            num_scalar_prefetch=2, grid=(m // rows,),
            in_specs=[row(d), row(LANES), pl.BlockSpec(memory_space=pl.ANY)],
            out_specs=row(d),
            scratch_shapes=[pltpu.VMEM((2, rows, d), F32), pltpu.SemaphoreType.DMA((2,))]),
        compiler_params=_cparams("arbitrary"),
        name="moe_combine",
    )(slot0, slot1, x_mid, route, yb)


def _moe_plan(route, tm):
    m = route.shape[0]
    a_tot = m * TOP_K
    e_flat = route[:, :TOP_K].astype(jnp.int32).reshape(a_tot)
    tok_flat = jnp.arange(a_tot, dtype=jnp.int32) // TOP_K
    onehot = (e_flat[:, None] == jnp.arange(N_EXPERTS, dtype=jnp.int32)[None, :]).astype(jnp.int32)
    csum = jnp.cumsum(onehot, axis=0)
    rank = jnp.sum((csum - onehot) * onehot, axis=1)
    counts = csum[-1]
    padded = (counts + tm - 1) // tm * tm
    pad_end = jnp.cumsum(padded)
    pad_start = pad_end - padded
    slot = (jnp.sum(pad_start[None, :] * onehot, axis=1) + rank).astype(jnp.int32)
    n_blocks = -(-(a_tot + N_EXPERTS * (tm - 1)) // tm)
    slot_tok = jnp.zeros((n_blocks * tm,), jnp.int32).at[slot].set(tok_flat)
    blk_start = (jnp.arange(n_blocks, dtype=jnp.int32) * tm)[:, None]
    block_e = jnp.minimum(jnp.sum((pad_end[None, :] <= blk_start).astype(jnp.int32), axis=1), N_EXPERTS - 1)
    slot_of = slot.reshape(m, TOP_K)
    return slot_tok, block_e.astype(jnp.int32), slot_of[:, 0], slot_of[:, 1]


def _ones_block():
    i = jnp.arange(LANES) // HEAD_DIM
    return (i[:, None] == i[None, :]).astype(BF16)


def kernel(x_prompt, x_sample, state_wkv, state_shift, cache_swa_k, cache_swa_v, norm_mix, norm_ffn, w_in, mu_shift, w0, w2, a0, a2, g2, k_k, k_a, r_k, ln_x_w, ln_x_b, v0, v1, v2, q_norm, k_norm, sinks, w_out, w_gate_d, w_up_d, w_down_d, w_router, b_router, w_gate_e, w_up_e, w_down_e):
    depth = w_in.shape[0]
    bp, tp, d = x_prompt.shape
    bs, ts, _ = x_sample.shape
    mp, ms = bp * tp, bs * ts
    m = mp + ms
    ones128 = _ones_block()
    xs = (x_prompt.reshape(mp, d), x_sample.reshape(ms, d))
    row = lambda a: a.reshape(1, -1).astype(F32)

    shift_zero = jnp.zeros((bp, N_SHIFT), F32)
    wkv_zero = jnp.zeros((bp, H_RWKV, HEAD_DIM, HEAD_DIM), F32)
    vfirst_p = vfirst_s = None
    outs = {k_: [] for k_ in ("wkv_p", "wkv_s", "sh_p", "sh_s", "k_p", "v_p", "k_s", "v_s")}

    for l in range(depth):
        pr, qa = _norm_proj(xs, norm_mix[l], w_in[l].astype(BF16))
        outs["sh_p"].append(pr[tp - 1:mp:tp])
        outs["sh_s"].append(pr[mp + ts - 1::ts])

        wwa = jnp.zeros((LANES, 2 * C_RWKV), F32)
        wwa = wwa.at[:D_DECAY, :C_RWKV].set(w2[l]).at[D_DECAY:, C_RWKV:].set(a2[l]).astype(BF16)
        prm = dict(mu=row(mu_shift[l]), w0=row(w0[l]), wwa=wwa, a0=row(a0[l]), g2=g2[l].astype(BF16),
                   k_k=row(k_k[l]), k_a=row(k_a[l]), r_k=row(r_k[l]))
        if l > 0:
            d_mv = v1.shape[2]
            prm["v0"] = row(v0[l - 1])
            prm["v1"] = jnp.zeros((C_RWKV, LANES), F32).at[:, :d_mv].set(v1[l - 1]).astype(BF16)
            prm["v2"] = jnp.zeros((LANES, C_RWKV), F32).at[:d_mv].set(v2[l - 1]).astype(BF16)
        r_p, lw_p, k_p, v_p, an_p, b_p, g_p, bonus_p = _rwkv_prep(pr, 0, bp, tp, shift_zero, prm, vfirst_p, ones128)
        r_s, lw_s, k_s, v_s, an_s, b_s, g_s, bonus_s = _rwkv_prep(pr, mp, bs, ts, state_shift[l], prm, vfirst_s,
                                                                 ones128)
        if l == 0:
            vfirst_p, vfirst_s = v_p, v_s

        y_p, wkv_p = _rwkv_scan(r_p, lw_p, k_p, v_p, an_p, b_p, wkv_zero, tp)
        y_s, wkv_s = _rwkv_scan(r_s, lw_s, k_s, v_s, an_s, b_s, state_wkv[l], ts)
        outs["wkv_p"].append(wkv_p)
        outs["wkv_s"].append(wkv_s)

        qn = jnp.tile(row(q_norm[l]), (1, H_ATTN))
        kn = jnp.tile(row(k_norm[l]), (1, KV_HEADS))
        sk = sinks[l].astype(F32)
        oa_p, kc_p = _swa_prompt(qa, bp, tp, sk, qn, kn, ones128)
        ck = cache_swa_k[l].reshape(bs, WINDOW, C_KV)
        cv = cache_swa_v[l].reshape(bs, WINDOW, C_KV)
        oa_s, nk_s, nv_s = _swa_sample(qa, mp, bs, ts, ck, cv, sk, qn, kn, ones128)
        outs["k_p"].append(kc_p.reshape(bp, WINDOW, KV_HEADS, HEAD_DIM))
        v_tail = qa[:mp].reshape(bp, tp, N_QKV)[:, tp - WINDOW:, C_ATTN + C_KV:]
        outs["v_p"].append(v_tail.reshape(bp, WINDOW, KV_HEADS, HEAD_DIM))
        outs["k_s"].append(nk_s.reshape(bs, WINDOW, KV_HEADS, HEAD_DIM))
        outs["v_s"].append(nv_s.reshape(bs, WINDOW, KV_HEADS, HEAD_DIM))

        wo = w_out[l].astype(BF16)
        moe = l % 2 == 1
        j = l // 2
        router = None
        if moe:
            wr = jnp.zeros((d, LANES), F32).at[:, :N_EXPERTS].set(w_router[j])
            wr_hi = wr.astype(BF16)
            wr_lo = (wr - wr_hi.astype(F32)).astype(BF16)
            br = jnp.zeros((1, LANES), F32).at[0, :N_EXPERTS].set(b_router[j])
            router = (wr_hi, wr_lo, br)
        post = _post(xs, [(y_p, y_s), (bonus_p, bonus_s), (g_p, g_s), (oa_p, oa_s)],
                     row(ln_x_w[l]), row(ln_x_b[l]), wo[:C_RWKV], wo[C_RWKV:], row(norm_ffn[l]), ones128,
                     router, F32 if moe else BF16)
        if not moe:
            x_mid, h = post
            tm = _pick_tile(m, (512, 256, 128, 64, 32, 16, 8))
            x = _ffn(h, jnp.zeros((m // tm,), jnp.int32), w_gate_d[j][None].astype(BF16),
                     w_up_d[j][None].astype(BF16), w_down_d[j][None].astype(BF16), x_mid, tm)
        else:
            x_mid, h, route = post
            tm = 512 if m >= 4096 else 16
            slot_tok, block_e, slot0, slot1 = _moe_plan(route, tm)
            xg = _gather_rows(h, slot_tok, tm)
            yb = _ffn(xg, block_e, w_gate_e[j].astype(BF16), w_up_e[j].astype(BF16), w_down_e[j].astype(BF16),
                      None, tm)
            x = _moe_combine(x_mid, route, yb, slot0, slot1)
        xs = (x,)

    st = lambda name: jnp.stack(outs[name])
    return (x[:mp].reshape(bp, tp, d), x[mp:].reshape(bs, ts, d), st("wkv_p"), st("wkv_s"), st("sh_p"), st("sh_s"),
            st("k_p"), st("v_p"), st("k_s"), st("v_s"))
```

```python
import functools

import jax
import jax.numpy as jnp
from jax import lax
from jax.experimental import pallas as pl
from jax.experimental.pallas import tpu as pltpu

F32 = jnp.float32
BF16 = jnp.bfloat16

HEAD_DIM = 64
C_RWKV = 512
H_RWKV = C_RWKV // HEAD_DIM
C_ATTN = 512
H_ATTN = C_ATTN // HEAD_DIM
KV_HEADS = 2
GQA_GROUP = H_ATTN // KV_HEADS
C_KV = KV_HEADS * HEAD_DIM
WINDOW = 128
ATTN_BLOCK = 128
D_DECAY = 64
D_AAA = 64
D_GATE = 128
N_SHIFT = 3 * C_RWKV + D_DECAY + D_AAA + D_GATE
N_QKV = C_ATTN + 2 * C_KV
N_EXPERTS = 8
TOP_K = 2
RMS_EPS = 1e-6
LNX_EPS = 64e-5
NEG = -1e30
LANES = 128
FF_CHUNK = 256
DMA_UNROLL = 8
VMEM_LIMIT = 56 * 1024 * 1024


def _cparams(*sem):
    return pltpu.CompilerParams(dimension_semantics=sem, vmem_limit_bytes=VMEM_LIMIT)


def _pick_tile(n, prefs):
    for t in prefs:
        if n % t == 0:
            return t
    return n


def _const_spec(shape):
    nd = len(shape)
    return pl.BlockSpec(shape, lambda *_: (0,) * nd)


def _dual_specs(tm, width, n_first):
    return [pl.BlockSpec((tm, width), lambda i, *_: (jnp.minimum(i, n_first - 1), 0)),
            pl.BlockSpec((tm, width), lambda i, *_: (jnp.maximum(i - n_first, 0), 0))]


def _pick_rows(a_ref, b_ref, n_first):
    return jnp.where(pl.program_id(0) < n_first, a_ref[...], b_ref[...])


def _split2(x):
    hi = x.astype(BF16)
    lo = (x - hi.astype(F32)).astype(BF16)
    return hi, lo


def _dot(a, b):
    return jnp.dot(a, b, preferred_element_type=F32)


def _dot_nt(a, b):
    return lax.dot_general(a, b, (((1,), (1,)), ((), ())), preferred_element_type=F32)


def _dot_tn(a, b):
    return lax.dot_general(a, b, (((0,), (0,)), ((), ())), preferred_element_type=F32)


def _bf(x):
    return x.astype(BF16)


def _head_sum(x, ones_ref):
    ones = ones_ref[...]
    outs = []
    for c in range(x.shape[1] // LANES):
        hi, lo = _split2(x[:, c * LANES:(c + 1) * LANES])
        outs.append(_dot(hi, ones) + _dot(lo, ones))
    return outs[0] if len(outs) == 1 else jnp.concatenate(outs, axis=1)


def _sigmoid(x):
    return 1.0 / (1.0 + jnp.exp(-x))


def _rms_rows(x, g):
    return x * lax.rsqrt(jnp.mean(x * x, axis=-1, keepdims=True) + RMS_EPS) * g


def _norm_proj_kernel(*refs, n_first):
    if n_first is None:
        x_ref, g_ref, w_ref, pr_ref, qa_ref = refs
        x = x_ref[...]
    else:
        xa_ref, xb_ref, g_ref, w_ref, pr_ref, qa_ref = refs
        x = _pick_rows(xa_ref, xb_ref, n_first)
    h = _rms_rows(x, g_ref[...])
    y = _dot(_bf(h), w_ref[...])
    pr_ref[...] = y[:, :N_SHIFT]
    qa_ref[...] = y[:, N_SHIFT:]


def _norm_proj(xs, g, w_bf16):
    m = sum(a.shape[0] for a in xs)
    d = xs[0].shape[1]
    n_in = w_bf16.shape[1]
    tm = _pick_tile(xs[-1].shape[0] if len(xs) == 2 else m, (512, 256, 128, 64, 32, 16, 8))
    if len(xs) == 2:
        n_first = xs[0].shape[0] // tm
        x_specs = _dual_specs(tm, d, n_first)
    else:
        n_first = None
        x_specs = [pl.BlockSpec((tm, d), lambda i: (i, 0))]
    return pl.pallas_call(
        functools.partial(_norm_proj_kernel, n_first=n_first),
        out_shape=(jax.ShapeDtypeStruct((m, N_SHIFT), F32), jax.ShapeDtypeStruct((m, N_QKV), F32)),
        grid=(m // tm,),
        in_specs=x_specs + [_const_spec((1, d)), _const_spec((d, n_in))],
        out_specs=(pl.BlockSpec((tm, N_SHIFT), lambda i: (i, 0)), pl.BlockSpec((tm, N_QKV), lambda i: (i, 0))),
        compiler_params=_cparams("parallel"),
        name="norm_proj",
    )(*xs, g.reshape(1, d), w_bf16)


def _rwkv_prep_kernel(*refs, has_vfirst, bs, tt):
    if has_vfirst:
        (pr_ref, shift_ref, mu_ref, w0_ref, wwa_ref, a0_ref, g2_ref, kk_ref, ka_ref, rk_ref, ones_ref,
         vf_ref, v0_ref, v1_ref, v2_ref,
         r_out, lw_out, k_out, v_out, an_out, b_out, g_out, bonus_out, carry_ref) = refs
    else:
        (pr_ref, shift_ref, mu_ref, w0_ref, wwa_ref, a0_ref, g2_ref, kk_ref, ka_ref, rk_ref, ones_ref,
         r_out, lw_out, k_out, v_out, an_out, b_out, g_out, bonus_out, carry_ref) = refs
    rows, n = pr_ref.shape
    j = pl.program_id(1)

    x = pr_ref[...]
    first3 = jnp.where(j == 0, shift_ref[...], carry_ref[...])
    carry_ref[...] = x.reshape(bs, tt, n)[:, tt - 1:tt, :]
    first = jnp.broadcast_to(first3, (bs, tt, n)).reshape(rows, n)
    rolled = pltpu.roll(x, 1, 0)
    tpos = lax.broadcasted_iota(jnp.int32, (rows, n), 0) % tt
    prev = jnp.where(tpos == 0, first, rolled)
    prm = x + (prev - x) * mu_ref[...]

    r = prm[:, 0:C_RWKV]
    k = prm[:, C_RWKV:2 * C_RWKV]
    v = prm[:, 2 * C_RWKV:3 * C_RWKV]
    wa = prm[:, 3 * C_RWKV:3 * C_RWKV + LANES]
    gl = prm[:, 3 * C_RWKV + LANES:3 * C_RWKV + 2 * LANES]

    lane = lax.broadcasted_iota(jnp.int32, wa.shape, 1)
    wa_in = _bf(jnp.where(lane < D_DECAY, jnp.tanh(wa), wa))
    wa_out = _dot(wa_in, wwa_ref[...])
    z = w0_ref[...] + wa_out[:, :C_RWKV]
    softplus = jnp.maximum(-z, 0.0) + jnp.log(1.0 + jnp.exp(-jnp.abs(z)))
    lw = -jnp.exp(-softplus - 0.5)
    a = _sigmoid(a0_ref[...] + wa_out[:, C_RWKV:])
    g = _dot(_bf(_sigmoid(gl)), g2_ref[...])

    if has_vfirst:
        t1 = _dot(_bf(v), v1_ref[...])
        t2 = _dot(_bf(t1), v2_ref[...])
        v = v + (vf_ref[...] - v) * _sigmoid(v0_ref[...] + t2)

    kk = k * kk_ref[...]
    norm = jnp.maximum(jnp.sqrt(_head_sum(kk * kk, ones_ref)), 1e-12)
    kk = kk / norm
    k = k * (1.0 + (a - 1.0) * ka_ref[...])
    bonus = _head_sum(r * k * rk_ref[...], ones_ref) * v

    r_out[...] = r
    lw_out[...] = lw
    k_out[...] = k
    v_out[...] = v
    an_out[...] = -kk
    b_out[...] = kk * a
    g_out[...] = g
    bonus_out[...] = bonus


def _rwkv_prep(pr, row0, nseq, t, shift, p, vfirst, ones128):
    n = pr.shape[1]
    if t >= 64:
        bs, tt = 1, _pick_tile(t, (256, 128, 64))
    else:
        bs, tt = _pick_tile(nseq, (16, 8, 4, 2, 1)), t
    rows = bs * tt
    nt = t // tt
    blk0 = row0 // rows
    ins = [pr, shift.reshape(nseq, 1, n), p["mu"], p["w0"], p["wwa"], p["a0"], p["g2"], p["k_k"], p["k_a"],
           p["r_k"], ones128]
    out_spec = pl.BlockSpec((rows, C_RWKV), lambda i, j: (i * nt + j, 0))
    specs = [pl.BlockSpec((rows, n), lambda i, j: (blk0 + i * nt + j, 0)),
             pl.BlockSpec((bs, 1, n), lambda i, j: (i, 0, 0))] + [_const_spec(a.shape) for a in ins[2:]]
    has_vfirst = vfirst is not None
    if has_vfirst:
        extra = [vfirst, p["v0"], p["v1"], p["v2"]]
        ins += extra
        specs += [out_spec] + [_const_spec(a.shape) for a in extra[1:]]
    out_sds = jax.ShapeDtypeStruct((nseq * t, C_RWKV), F32)
    return pl.pallas_call(
        functools.partial(_rwkv_prep_kernel, has_vfirst=has_vfirst, bs=bs, tt=tt),
        out_shape=(out_sds,) * 8,
        grid=(nseq // bs, nt),
        in_specs=specs,
        out_specs=(out_spec,) * 8,
        scratch_shapes=[pltpu.VMEM((bs, 1, n), F32)],
        compiler_params=_cparams("parallel", "arbitrary"),
        name="rwkv_prep",
    )(*ins)


def _scan_chunk(r, lw, k, v, an, b, s_list, chunk):
    C = chunk
    H = range(H_RWKV)
    row = lax.broadcasted_iota(jnp.int32, (C, C), 0)
    col = lax.broadcasted_iota(jnp.int32, (C, C), 1)
    incl = row >= col
    strict = row > col
    tri = _bf(jnp.where(incl, 1.0, 0.0))
    l1 = _bf(lw)
    rem = lw - l1.astype(F32)
    l2 = _bf(rem)
    l3 = _bf(rem - l2.astype(F32))
    cum = _dot(tri, l1) + _dot(tri, l2) + _dot(tri, l3)
    cum_last = cum[C - 1:C, :]
    e_neg = jnp.exp(-cum)
    e_d = jnp.exp(cum_last - cum)
    pc = jnp.exp(cum_last)
    eye = jnp.where(row == col, 1.0, 0.0)
    hs = lambda x, h: x[:, h * HEAD_DIM:(h + 1) * HEAD_DIM]

    lhs = _bf(jnp.concatenate([an * jnp.exp(cum - lw), r * jnp.exp(cum)], axis=0))
    bq, kq = _bf(b * e_neg), _bf(k * e_neg)
    bd, kd = _bf(b * e_d), _bf(k * e_d)
    vb = _bf(v)
    m_b = [_dot_nt(hs(lhs, h), hs(bq, h)) for h in H]
    m_k = [_dot_nt(hs(lhs, h), hs(kq, h)) for h in H]
    a_ab = [jnp.where(strict, m[:C], 0.0) for m in m_b]
    a_rb = [_bf(jnp.where(incl, m[C:], 0.0)) for m in m_b]
    row2 = lax.broadcasted_iota(jnp.int32, (2 * C, C), 0)
    col2 = lax.broadcasted_iota(jnp.int32, (2 * C, C), 1)
    mask2 = jnp.where(row2 < C, row2, row2 - C + 1) > col2
    a_k = [_bf(jnp.where(mask2, m, 0.0)) for m in m_k]
    kv = [_dot(a_k[h], hs(vb, h)) for h in H]
    skv = [_dot_tn(hs(vb, h), hs(kd, h)) for h in H]

    t_inv = [eye + a for a in a_ab]
    n_sq = max(C.bit_length() - 2, 0)
    if n_sq:
        a_b = [_bf(a) for a in a_ab]
        p_pow = [_dot(a_b[h], a_b[h]) for h in H]
        for lev in range(n_sq):
            p_b = [_bf(p) for p in p_pow]
            t_inv = [t_inv[h] + _dot(_bf(t_inv[h]), p_b[h]) for h in H]
            if lev + 1 < n_sq:
                p_pow = [_dot(p_b[h], p_b[h]) for h in H]
    t_b = [_bf(t) for t in t_inv]
    w1 = [_dot(t_b[h], hs(lhs[:C], h)) for h in H]
    u0 = [_dot(t_b[h], _bf(kv[h][:C])) for h in H]
    s_b = [_bf(s) for s in s_list]
    x1 = [_dot_nt(jnp.concatenate([_bf(w1[h]), hs(lhs[C:], h)], axis=0), s_b[h]) for h in H]
    u_b = [_bf(x1[h][:C] + u0[h]) for h in H]
    y = [x1[h][C:] + _dot(a_rb[h], u_b[h]) + kv[h][C:] for h in H]
    s_new = [s_list[h] * hs(pc, h) + _dot_tn(u_b[h], hs(bd, h)) + skv[h] for h in H]
    return jnp.concatenate(y, axis=1), s_new


def _scan_kernel(r_ref, lw_ref, k_ref, v_ref, an_ref, b_ref, s0_ref, y_ref, sout_ref, s_scr, *, chunk, nb):
    c = pl.program_id(1)

    @pl.when(c == 0)
    def _():
        s_scr[...] = s0_ref[...]

    def body(i, carry):
        rows = pl.ds(0, chunk) if nb == 1 else pl.ds(pl.multiple_of(i * chunk, chunk), chunk)
        s_list = [s_scr[i, h] for h in range(H_RWKV)]
        y, s_new = _scan_chunk(r_ref[rows, :], lw_ref[rows, :], k_ref[rows, :], v_ref[rows, :], an_ref[rows, :],
                               b_ref[rows, :], s_list, chunk)
        y_ref[rows, :] = y
        for h in range(H_RWKV):
            s_scr[i, h] = s_new[h]
        return carry

    if nb == 1:
        body(0, 0)
    else:
        lax.fori_loop(0, nb, body, 0)

    @pl.when(c == pl.num_programs(1) - 1)
    def _():
        sout_ref[...] = s_scr[...]


def _rwkv_scan(r, lw, k, v, an, b, s0, t):
    nseq = s0.shape[0]
    chunk = _pick_tile(t, (64, 32, 16, 8))
    nb = 1 if t > chunk else _pick_tile(nseq, (8, 4, 2, 1))
    nc = t // chunk
    seq_spec = pl.BlockSpec((nb * chunk, C_RWKV), lambda i, j: (i * nc + j, 0))
    st_spec = pl.BlockSpec((nb, H_RWKV, HEAD_DIM, HEAD_DIM), lambda i, j: (i, 0, 0, 0))
    return pl.pallas_call(
        functools.partial(_scan_kernel, chunk=chunk, nb=nb),
        out_shape=(jax.ShapeDtypeStruct((nseq * t, C_RWKV), F32),
                   jax.ShapeDtypeStruct((nseq, H_RWKV, HEAD_DIM, HEAD_DIM), F32)),
        grid=(nseq // nb, nc),
        in_specs=[seq_spec] * 6 + [st_spec],
        out_specs=(seq_spec, st_spec),
        scratch_shapes=[pltpu.VMEM((nb, H_RWKV, HEAD_DIM, HEAD_DIM), F32)],
        compiler_params=_cparams("parallel", "arbitrary"),
        name="rwkv_scan",
    )(r, lw, k, v, an, b, s0)


def _qk_norm(x, g, ones_ref):
    ms = _head_sum(x * x, ones_ref) * (1.0 / HEAD_DIM)
    return x * lax.rsqrt(ms + RMS_EPS) * g


def _sink_softmax_rows(s_parts, sink):
    m = sink
    for s in s_parts:
        m = jnp.maximum(m, jnp.max(s, axis=-1, keepdims=True))
    ps = [jnp.exp(s - m) for s in s_parts]
    den = jnp.exp(sink - m)
    for p in ps:
        den = den + jnp.sum(p, axis=-1, keepdims=True)
    inv = 1.0 / den
    return [p * inv for p in ps]


def _group_rows(q, g, t):
    return jnp.concatenate(
        [q[:, (g * GQA_GROUP + a) * HEAD_DIM:(g * GQA_GROUP + a + 1) * HEAD_DIM] for a in range(GQA_GROUP)], axis=0)


def _group_sinks(sink_ref, g, rows, t):
    rowh = lax.broadcasted_iota(jnp.int32, (rows, 1), 0) // t
    sink = jnp.zeros((rows, 1), F32)
    for a in range(GQA_GROUP):
        sink = jnp.where(rowh == a, sink_ref[g * GQA_GROUP + a], sink)
    return sink


def _swa_prompt_kernel(sink_ref, cur_ref, prev_ref, qn_ref, kn_ref, ones_ref, o_ref, kc_ref):
    n = pl.program_id(1)
    L = ATTN_BLOCK
    rows = GQA_GROUP * L
    cur = cur_ref[...]
    prev = prev_ref[...]
    q = _bf(_qk_norm(cur[:, :C_ATTN], qn_ref[...], ones_ref))
    k_cur = _qk_norm(cur[:, C_ATTN:C_ATTN + C_KV], kn_ref[...], ones_ref)
    k_prev = _qk_norm(prev[:, C_ATTN:C_ATTN + C_KV], kn_ref[...], ones_ref)
    kc_ref[0] = k_cur
    kk = _bf(jnp.concatenate([k_prev, k_cur], axis=0))
    vv = _bf(jnp.concatenate([prev[:, C_ATTN + C_KV:], cur[:, C_ATTN + C_KV:]], axis=0))
    qq = lax.broadcasted_iota(jnp.int32, (rows, 2 * L), 0) % L
    jj = lax.broadcasted_iota(jnp.int32, (rows, 2 * L), 1)
    mask = (jj > qq) & (jj <= qq + L) & ((jj >= L) | (n > 0))
    G = range(KV_HEADS)
    ksl = lambda x, g: x[:, g * HEAD_DIM:(g + 1) * HEAD_DIM]
    s = [jnp.where(mask, _dot_nt(_group_rows(q, g, L), ksl(kk, g)) * (HEAD_DIM ** -0.5), NEG) for g in G]
    p = [_sink_softmax_rows([s[g]], _group_sinks(sink_ref, g, rows, L))[0] for g in G]
    o = [_dot(_bf(p[g]), ksl(vv, g)) for g in G]
    o_ref[...] = jnp.concatenate([o[g][a * L:(a + 1) * L, :] for g in G for a in range(GQA_GROUP)], axis=1)


def _swa_prompt(qa, bsz, t, sinks, qn, kn, ones128):
    L = ATTN_BLOCK
    nb = t // L
    return pl.pallas_call(
        _swa_prompt_kernel,
        out_shape=(jax.ShapeDtypeStruct((bsz * t, C_ATTN), F32), jax.ShapeDtypeStruct((bsz, L, C_KV), F32)),
        grid_spec=pltpu.PrefetchScalarGridSpec(
            num_scalar_prefetch=1,
            grid=(bsz, nb),
            in_specs=[pl.BlockSpec((L, N_QKV), lambda b, n, s: (b * nb + n, 0)),
                      pl.BlockSpec((L, N_QKV), lambda b, n, s: (b * nb + jnp.maximum(n - 1, 0), 0)),
                      pl.BlockSpec((1, C_ATTN), lambda b, n, s: (0, 0)),
                      pl.BlockSpec((1, C_KV), lambda b, n, s: (0, 0)),
                      pl.BlockSpec((LANES, LANES), lambda b, n, s: (0, 0))],
            out_specs=(pl.BlockSpec((L, C_ATTN), lambda b, n, s: (b * nb + n, 0)),
                       pl.BlockSpec((1, L, C_KV), lambda b, n, s: (b, 0, 0)))),
        compiler_params=_cparams("parallel", "arbitrary"),
        name="swa_prompt",
    )(sinks, qa, qa, qn, kn, ones128)


def _swa_sample_kernel(sink_ref, qa_ref, ck_ref, cv_ref, qn_ref, kn_ref, ones_ref, o_ref, nk_ref, nv_ref, *, bb, t):
    W = WINDOW
    rows = GQA_GROUP * t
    rr = lax.broadcasted_iota(jnp.int32, (rows, W), 0) % t
    jc = lax.broadcasted_iota(jnp.int32, (rows, W), 1)
    mask_c = jc > rr
    rn = lax.broadcasted_iota(jnp.int32, (rows, t), 0) % t
    jn = lax.broadcasted_iota(jnp.int32, (rows, t), 1)
    mask_n = jn <= rn
    G = range(KV_HEADS)
    ksl = lambda x, g: x[:, g * HEAD_DIM:(g + 1) * HEAD_DIM]
    scale = HEAD_DIM ** -0.5

    def body(i, carry):
        x = qa_ref[pl.ds(pl.multiple_of(i * t, t), t), :]
        q = _bf(_qk_norm(x[:, :C_ATTN], qn_ref[...], ones_ref))
        k_new = _qk_norm(x[:, C_ATTN:C_ATTN + C_KV], kn_ref[...], ones_ref)
        v_new = x[:, C_ATTN + C_KV:]
        ck = ck_ref[i]
        cv = cv_ref[i]
        nk_ref[i, 0:W - t, :] = ck[t:, :]
        nk_ref[i, W - t:W, :] = k_new
        nv_ref[i, 0:W - t, :] = cv[t:, :]
        nv_ref[i, W - t:W, :] = v_new
        ckb, cvb, knb, vnb = _bf(ck), _bf(cv), _bf(k_new), _bf(v_new)
        qg = [_group_rows(q, g, t) for g in G]
        s_c = [jnp.where(mask_c, _dot_nt(qg[g], ksl(ckb, g)) * scale, NEG) for g in G]
        s_n = [jnp.where(mask_n, _dot_nt(qg[g], ksl(knb, g)) * scale, NEG) for g in G]
        p = [_sink_softmax_rows([s_c[g], s_n[g]], _group_sinks(sink_ref, g, rows, t)) for g in G]
        og = [_dot(_bf(p[g][0]), ksl(cvb, g)) + _dot(_bf(p[g][1]), ksl(vnb, g)) for g in G]
        o_ref[pl.ds(pl.multiple_of(i * t, t), t), :] = jnp.concatenate(
            [og[g][a * t:(a + 1) * t, :] for g in G for a in range(GQA_GROUP)], axis=1)
        return carry

    lax.fori_loop(0, bb, body, 0)


def _swa_sample(qa, row0, bsz, t, ck, cv, sinks, qn, kn, ones128):
    W = WINDOW
    bb = _pick_tile(bsz, (16, 8, 4, 2, 1))
    blk0 = row0 // (bb * t)
    sds = jax.ShapeDtypeStruct
    return pl.pallas_call(
        functools.partial(_swa_sample_kernel, bb=bb, t=t),
        out_shape=(sds((bsz * t, C_ATTN), F32), sds((bsz, W, C_KV), F32), sds((bsz, W, C_KV), F32)),
        grid_spec=pltpu.PrefetchScalarGridSpec(
            num_scalar_prefetch=1,
            grid=(bsz // bb,),
            in_specs=[pl.BlockSpec((bb * t, N_QKV), lambda b, s: (blk0 + b, 0)),
                      pl.BlockSpec((bb, W, C_KV), lambda b, s: (b, 0, 0)),
                      pl.BlockSpec((bb, W, C_KV), lambda b, s: (b, 0, 0)),
                      pl.BlockSpec((1, C_ATTN), lambda b, s: (0, 0)),
                      pl.BlockSpec((1, C_KV), lambda b, s: (0, 0)),
                      pl.BlockSpec((LANES, LANES), lambda b, s: (0, 0))],
            out_specs=(pl.BlockSpec((bb * t, C_ATTN), lambda b, s: (b, 0)),
                       pl.BlockSpec((bb, W, C_KV), lambda b, s: (b, 0, 0)),
                       pl.BlockSpec((bb, W, C_KV), lambda b, s: (b, 0, 0)))),
        compiler_params=_cparams("parallel"),
        name="swa_sample",
    )(sinks, qa, ck, cv, qn, kn, ones128)


def _post_kernel(*refs, has_router, n_first, n_x):
    xs, refs = refs[:n_x], refs[n_x:]
    pairs, refs = refs[:8], refs[8:]
    if has_router:
        (lnw_ref, lnb_ref, wt_ref, wb_ref, gf_ref, ones_ref, wrh_ref, wrl_ref, br_ref,
         xmid_ref, h_ref, route_ref) = refs
    else:
        lnw_ref, lnb_ref, wt_ref, wb_ref, gf_ref, ones_ref, xmid_ref, h_ref = refs
    x = xs[0][...] if n_x == 1 else _pick_rows(xs[0], xs[1], n_first)
    y, bonus, g, o_a = [_pick_rows(pairs[2 * i], pairs[2 * i + 1], n_first) for i in range(4)]
    mu = _head_sum(y, ones_ref) * (1.0 / HEAD_DIM)
    d = y - mu
    var = _head_sum(d * d, ones_ref) * (1.0 / HEAD_DIM)
    yn = d * lax.rsqrt(var + LNX_EPS) * lnw_ref[...] + lnb_ref[...]
    o_r = (yn + bonus) * g
    mixed = _dot(_bf(o_r), wt_ref[...]) + _dot(_bf(o_a), wb_ref[...])
    x_mid = x + mixed
    xmid_ref[...] = x_mid
    h = _rms_rows(x_mid, gf_ref[...])
    h_ref[...] = h.astype(h_ref.dtype)
    if has_router:
        hh, hl = _split2(h)
        logits = _dot(hh, wrh_ref[...]) + _dot(hh, wrl_ref[...]) + _dot(hl, wrh_ref[...]) + br_ref[...]
        lane = lax.broadcasted_iota(jnp.int32, logits.shape, 1)
        lg = jnp.where(lane < N_EXPERTS, logits, -jnp.inf)
        m1 = jnp.max(lg, axis=-1, keepdims=True)
        i1 = jnp.min(jnp.where(lg == m1, lane, LANES), axis=-1, keepdims=True)
        lg2 = jnp.where(lane == i1, -jnp.inf, lg)
        m2 = jnp.max(lg2, axis=-1, keepdims=True)
        i2 = jnp.min(jnp.where(lg2 == m2, lane, LANES), axis=-1, keepdims=True)
        e = jnp.exp(m2 - m1)
        g1 = 1.0 / (1.0 + e)
        g2 = e * g1
        route_ref[...] = jnp.where(lane == 0, i1.astype(F32),
                                   jnp.where(lane == 1, i2.astype(F32),
                                             jnp.where(lane == 2, g1, jnp.where(lane == 3, g2, 0.0))))


def _post(xs, pairs, lnw, lnb, w_top, w_bot, g_ffn, ones128, router, h_dtype):
    d = xs[0].shape[1]
    m = sum(a.shape[0] for a in pairs[0])
    tm = _pick_tile(pairs[0][1].shape[0], (256, 128, 64, 32, 16, 8))
    n_first = pairs[0][0].shape[0] // tm
    row = lambda w: pl.BlockSpec((tm, w), lambda i: (i, 0))
    consts = [lnw, lnb, w_top, w_bot, g_ffn, ones128] + (list(router) if router is not None else [])
    ins = list(xs) + [a for pr_ in pairs for a in pr_] + consts
    specs = (_dual_specs(tm, d, n_first) if len(xs) == 2 else [row(d)])
    for _ in pairs:
        specs += _dual_specs(tm, C_RWKV, n_first)
    specs += [_const_spec(a.shape) for a in consts]
    outs = [jax.ShapeDtypeStruct((m, d), F32), jax.ShapeDtypeStruct((m, d), h_dtype)]
    out_specs = [row(d), row(d)]
    if router is not None:
        outs.append(jax.ShapeDtypeStruct((m, LANES), F32))
        out_specs.append(row(LANES))
    return pl.pallas_call(
        functools.partial(_post_kernel, has_router=router is not None, n_first=n_first, n_x=len(xs)),
        out_shape=tuple(outs),
        grid=(m // tm,),
        in_specs=specs,
        out_specs=tuple(out_specs),
        compiler_params=_cparams("parallel"),
        name="post_mix",
    )(*ins)


def _ffn_kernel(*refs, has_res):
    if has_res:
        be_ref, x_ref, wg_ref, wu_ref, wd_ref, res_ref, o_ref = refs
    else:
        be_ref, x_ref, wg_ref, wu_ref, wd_ref, o_ref = refs
    del be_ref
    x = _bf(x_ref[...])
    d_ff = wg_ref.shape[2]
    acc = None
    for f in range(d_ff // FF_CHUNK):
        fs = slice(f * FF_CHUNK, (f + 1) * FF_CHUNK)
        gt = _dot(x, wg_ref[0, :, fs])
        up = _dot(x, wu_ref[0, :, fs])
        act = _bf(gt * _sigmoid(gt) * up)
        part = _dot(act, wd_ref[0, fs, :])
        acc = part if acc is None else acc + part
    if has_res:
        acc = acc + res_ref[...]
    o_ref[...] = acc


def _ffn(x, block_e, wg, wu, wd, res, tm):
    m, d = x.shape
    d_ff = wg.shape[2]
    row = pl.BlockSpec((tm, d), lambda i, be: (i, 0))
    ins = [x, wg, wu, wd]
    specs = [row,
             pl.BlockSpec((1, d, d_ff), lambda i, be: (be[i], 0, 0)),
             pl.BlockSpec((1, d, d_ff), lambda i, be: (be[i], 0, 0)),
             pl.BlockSpec((1, d_ff, d), lambda i, be: (be[i], 0, 0))]
    if res is not None:
        ins.append(res)
        specs.append(row)
    return pl.pallas_call(
        functools.partial(_ffn_kernel, has_res=res is not None),
        out_shape=jax.ShapeDtypeStruct((m, d), F32),
        grid_spec=pltpu.PrefetchScalarGridSpec(
            num_scalar_prefetch=1, grid=(m // tm,), in_specs=specs, out_specs=row),
        compiler_params=_cparams("arbitrary"),
        name="swiglu",
    )(block_e, *ins)


def _gather_kernel(idx_ref, src_ref, o_ref, sem, *, rows):
    base = pl.program_id(0) * rows

    def start(g, carry):
        for u in range(DMA_UNROLL):
            r = g * DMA_UNROLL + u
            pltpu.make_async_copy(src_ref.at[pl.ds(idx_ref[base + r], 1)], o_ref.at[pl.ds(r, 1)],
                                  sem).start(priority=u % 2)
        return carry

    lax.fori_loop(0, rows // DMA_UNROLL, start, 0)
    pltpu.make_async_copy(src_ref.at[pl.ds(0, rows)], o_ref, sem).wait()


def _gather_rows(src, idx, rows):
    n = idx.shape[0]
    d = src.shape[1]
    return pl.pallas_call(
        functools.partial(_gather_kernel, rows=rows),
        out_shape=jax.ShapeDtypeStruct((n, d), src.dtype),
        grid_spec=pltpu.PrefetchScalarGridSpec(
            num_scalar_prefetch=1, grid=(n // rows,),
            in_specs=[pl.BlockSpec(memory_space=pl.ANY)],
            out_specs=pl.BlockSpec((rows, d), lambda i, idx: (i, 0)),
            scratch_shapes=[pltpu.SemaphoreType.DMA(())]),
        compiler_params=_cparams("arbitrary"),
        name="gather_rows",
    )(idx, src)


def _combine_kernel(s0_ref, s1_ref, x_ref, route_ref, yb_ref, o_ref, buf, sem, *, rows):
    base = pl.program_id(0) * rows

    def start(g, carry):
        for u in range(DMA_UNROLL // 2):
            r = g * (DMA_UNROLL // 2) + u
            pltpu.make_async_copy(yb_ref.at[pl.ds(s0_ref[base + r], 1)], buf.at[0, pl.ds(r, 1)],
                                  sem.at[0]).start(priority=0)
            pltpu.make_async_copy(yb_ref.at[pl.ds(s1_ref[base + r], 1)], buf.at[1, pl.ds(r, 1)],
                                  sem.at[1]).start(priority=1)
        return carry

    lax.fori_loop(0, rows // (DMA_UNROLL // 2), start, 0)
    pltpu.make_async_copy(yb_ref.at[pl.ds(0, rows)], buf.at[0], sem.at[0]).wait()
    pltpu.make_async_copy(yb_ref.at[pl.ds(0, rows)], buf.at[1], sem.at[1]).wait()
    route = route_ref[...]
    o_ref[...] = x_ref[...] + route[:, 2:3] * buf[0] + route[:, 3:4] * buf[1]


def _moe_combine(x_mid, route, yb, slot0, slot1):
    m, d = x_mid.shape
    rows = _pick_tile(m, (256, 128, 64, 32, 16, 8))
    row = lambda w: pl.BlockSpec((rows, w), lambda i, a, b: (i, 0))
    return pl.pallas_call(
        functools.partial(_combine_kernel, rows=rows),
        out_shape=jax.ShapeDtypeStruct((m, d), F32),
        grid_spec=pltpu.PrefetchScalarGridSpec(
            num_scalar_prefetch=2, grid=(m // rows,),
            in_specs=[row(d), row(LANES), pl.BlockSpec(memory_space=pl.ANY)],
            out_specs=row(d),
            scratch_shapes=[pltpu.VMEM((2, rows, d), F32), pltpu.SemaphoreType.DMA((2,))]),
        compiler_params=_cparams("arbitrary"),
        name="moe_combine",
    )(slot0, slot1, x_mid, route, yb)


def _moe_plan(route, tm):
    m = route.shape[0]
    a_tot = m * TOP_K
    e_flat = route[:, :TOP_K].astype(jnp.int32).reshape(a_tot)
    tok_flat = jnp.arange(a_tot, dtype=jnp.int32) // TOP_K
    onehot = (e_flat[:, None] == jnp.arange(N_EXPERTS, dtype=jnp.int32)[None, :]).astype(jnp.int32)
    csum = jnp.cumsum(onehot, axis=0)
    rank = jnp.sum((csum - onehot) * onehot, axis=1)
    counts = csum[-1]
    padded = (counts + tm - 1) // tm * tm
    pad_end = jnp.cumsum(padded)
    pad_start = pad_end - padded
    slot = (jnp.sum(pad_start[None, :] * onehot, axis=1) + rank).astype(jnp.int32)
    n_blocks = -(-(a_tot + N_EXPERTS * (tm - 1)) // tm)
    slot_tok = jnp.zeros((n_blocks * tm,), jnp.int32).at[slot].set(tok_flat)
    blk_start = (jnp.arange(n_blocks, dtype=jnp.int32) * tm)[:, None]
    block_e = jnp.minimum(jnp.sum((pad_end[None, :] <= blk_start).astype(jnp.int32), axis=1), N_EXPERTS - 1)
    slot_of = slot.reshape(m, TOP_K)
    return slot_tok, block_e.astype(jnp.int32), slot_of[:, 0], slot_of[:, 1]


def _ones_block():
    i = jnp.arange(LANES) // HEAD_DIM
    return (i[:, None] == i[None, :]).astype(BF16)


def kernel(x_prompt, x_sample, state_wkv, state_shift, cache_swa_k, cache_swa_v, norm_mix, norm_ffn, w_in, mu_shift, w0, w2, a0, a2, g2, k_k, k_a, r_k, ln_x_w, ln_x_b, v0, v1, v2, q_norm, k_norm, sinks, w_out, w_gate_d, w_up_d, w_down_d, w_router, b_router, w_gate_e, w_up_e, w_down_e):
    depth = w_in.shape[0]
    bp, tp, d = x_prompt.shape
    bs, ts, _ = x_sample.shape
    mp, ms = bp * tp, bs * ts
    m = mp + ms
    ones128 = _ones_block()
    xs = (x_prompt.reshape(mp, d), x_sample.reshape(ms, d))
    row = lambda a: a.reshape(1, -1).astype(F32)

    shift_zero = jnp.zeros((bp, N_SHIFT), F32)
    wkv_zero = jnp.zeros((bp, H_RWKV, HEAD_DIM, HEAD_DIM), F32)
    vfirst_p = vfirst_s = None
    outs = {k_: [] for k_ in ("wkv_p", "wkv_s", "sh_p", "sh_s", "k_p", "v_p", "k_s", "v_s")}

    for l in range(depth):
        pr, qa = _norm_proj(xs, norm_mix[l], w_in[l].astype(BF16))
        outs["sh_p"].append(pr[tp - 1:mp:tp])
        outs["sh_s"].append(pr[mp + ts - 1::ts])

        wwa = jnp.zeros((LANES, 2 * C_RWKV), F32)
        wwa = wwa.at[:D_DECAY, :C_RWKV].set(w2[l]).at[D_DECAY:, C_RWKV:].set(a2[l]).astype(BF16)
        prm = dict(mu=row(mu_shift[l]), w0=row(w0[l]), wwa=wwa, a0=row(a0[l]), g2=g2[l].astype(BF16),
                   k_k=row(k_k[l]), k_a=row(k_a[l]), r_k=row(r_k[l]))
        if l > 0:
            d_mv = v1.shape[2]
            prm["v0"] = row(v0[l - 1])
            prm["v1"] = jnp.zeros((C_RWKV, LANES), F32).at[:, :d_mv].set(v1[l - 1]).astype(BF16)
            prm["v2"] = jnp.zeros((LANES, C_RWKV), F32).at[:d_mv].set(v2[l - 1]).astype(BF16)
        r_p, lw_p, k_p, v_p, an_p, b_p, g_p, bonus_p = _rwkv_prep(pr, 0, bp, tp, shift_zero, prm, vfirst_p, ones128)
        r_s, lw_s, k_s, v_s, an_s, b_s, g_s, bonus_s = _rwkv_prep(pr, mp, bs, ts, state_shift[l], prm, vfirst_s,
                                                                 ones128)
        if l == 0:
            vfirst_p, vfirst_s = v_p, v_s

        y_p, wkv_p = _rwkv_scan(r_p, lw_p, k_p, v_p, an_p, b_p, wkv_zero, tp)
        y_s, wkv_s = _rwkv_scan(r_s, lw_s, k_s, v_s, an_s, b_s, state_wkv[l], ts)
        outs["wkv_p"].append(wkv_p)
        outs["wkv_s"].append(wkv_s)

        qn = jnp.tile(row(q_norm[l]), (1, H_ATTN))
        kn = jnp.tile(row(k_norm[l]), (1, KV_HEADS))
        sk = sinks[l].astype(F32)
        oa_p, kc_p = _swa_prompt(qa, bp, tp, sk, qn, kn, ones128)
        ck = cache_swa_k[l].reshape(bs, WINDOW, C_KV)
        cv = cache_swa_v[l].reshape(bs, WINDOW, C_KV)
        oa_s, nk_s, nv_s = _swa_sample(qa, mp, bs, ts, ck, cv, sk, qn, kn, ones128)
        outs["k_p"].append(kc_p.reshape(bp, WINDOW, KV_HEADS, HEAD_DIM))
        v_tail = qa[:mp].reshape(bp, tp, N_QKV)[:, tp - WINDOW:, C_ATTN + C_KV:]
        outs["v_p"].append(v_tail.reshape(bp, WINDOW, KV_HEADS, HEAD_DIM))
        outs["k_s"].append(nk_s.reshape(bs, WINDOW, KV_HEADS, HEAD_DIM))
        outs["v_s"].append(nv_s.reshape(bs, WINDOW, KV_HEADS, HEAD_DIM))

        wo = w_out[l].astype(BF16)
        moe = l % 2 == 1
        j = l // 2
        router = None
        if moe:
            wr = jnp.zeros((d, LANES), F32).at[:, :N_EXPERTS].set(w_router[j])
            wr_hi = wr.astype(BF16)
            wr_lo = (wr - wr_hi.astype(F32)).astype(BF16)
            br = jnp.zeros((1, LANES), F32).at[0, :N_EXPERTS].set(b_router[j])
            router = (wr_hi, wr_lo, br)
        post = _post(xs, [(y_p, y_s), (bonus_p, bonus_s), (g_p, g_s), (oa_p, oa_s)],
                     row(ln_x_w[l]), row(ln_x_b[l]), wo[:C_RWKV], wo[C_RWKV:], row(norm_ffn[l]), ones128,
                     router, F32 if moe else BF16)
        if not moe:
            x_mid, h = post
            tm = _pick_tile(m, (512, 256, 128, 64, 32, 16, 8))
            x = _ffn(h, jnp.zeros((m // tm,), jnp.int32), w_gate_d[j][None].astype(BF16),
                     w_up_d[j][None].astype(BF16), w_down_d[j][None].astype(BF16), x_mid, tm)
        else:
            x_mid, h, route = post
            tm = 512 if m >= 4096 else 16
            slot_tok, block_e, slot0, slot1 = _moe_plan(route, tm)
            xg = _gather_rows(h, slot_tok, tm)
            yb = _ffn(xg, block_e, w_gate_e[j].astype(BF16), w_up_e[j].astype(BF16), w_down_e[j].astype(BF16),
                      None, tm)
            x = _moe_combine(x_mid, route, yb, slot0, slot1)
        xs = (x,)

    st = lambda name: jnp.stack(outs[name])
    return (x[:mp].reshape(bp, tp, d), x[mp:].reshape(bs, ts, d), st("wkv_p"), st("wkv_s"), st("sh_p"), st("sh_s"),
            st("k_p"), st("v_p"), st("k_s"), st("v_s"))
```

```python
import functools

import jax
import jax.numpy as jnp
from jax import lax
from jax.experimental import pallas as pl
from jax.experimental.pallas import tpu as pltpu

F32 = jnp.float32
BF16 = jnp.bfloat16

HEAD_DIM = 64
C_RWKV = 512
H_RWKV = C_RWKV // HEAD_DIM
C_ATTN = 512
H_ATTN = C_ATTN // HEAD_DIM
KV_HEADS = 2
GQA_GROUP = H_ATTN // KV_HEADS
C_KV = KV_HEADS * HEAD_DIM
WINDOW = 128
ATTN_BLOCK = 128
D_DECAY = 64
D_AAA = 64
D_GATE = 128
N_SHIFT = 3 * C_RWKV + D_DECAY + D_AAA + D_GATE
N_QKV = C_ATTN + 2 * C_KV
N_EXPERTS = 8
TOP_K = 2
RMS_EPS = 1e-6
LNX_EPS = 64e-5
NEG = -1e30
LANES = 128
FF_CHUNK = 256
DMA_UNROLL = 8
VMEM_LIMIT = 56 * 1024 * 1024


def _cparams(*sem):
    return pltpu.CompilerParams(dimension_semantics=sem, vmem_limit_bytes=VMEM_LIMIT)


def _pick_tile(n, prefs):
    for t in prefs:
        if n % t == 0:
            return t
    return n


def _const_spec(shape):
    nd = len(shape)
    return pl.BlockSpec(shape, lambda *_: (0,) * nd)


def _dual_specs(tm, width, n_first):
    return [pl.BlockSpec((tm, width), lambda i, *_: (jnp.minimum(i, n_first - 1), 0)),
            pl.BlockSpec((tm, width), lambda i, *_: (jnp.maximum(i - n_first, 0), 0))]


def _pick_rows(a_ref, b_ref, n_first):
    return jnp.where(pl.program_id(0) < n_first, a_ref[...], b_ref[...])


def _split2(x):
    hi = x.astype(BF16)
    lo = (x - hi.astype(F32)).astype(BF16)
    return hi, lo


def _dot(a, b):
    return jnp.dot(a, b, preferred_element_type=F32)


def _dot_nt(a, b):
    return lax.dot_general(a, b, (((1,), (1,)), ((), ())), preferred_element_type=F32)


def _dot_tn(a, b):
    return lax.dot_general(a, b, (((0,), (0,)), ((), ())), preferred_element_type=F32)


def _bf(x):
    return x.astype(BF16)


def _head_sum(x, ones_ref):
    ones = ones_ref[...]
    outs = []
    for c in range(x.shape[1] // LANES):
        hi, lo = _split2(x[:, c * LANES:(c + 1) * LANES])
        outs.append(_dot(hi, ones) + _dot(lo, ones))
    return outs[0] if len(outs) == 1 else jnp.concatenate(outs, axis=1)


def _sigmoid(x):
    return 1.0 / (1.0 + jnp.exp(-x))


def _rms_rows(x, g):
    return x * lax.rsqrt(jnp.mean(x * x, axis=-1, keepdims=True) + RMS_EPS) * g


def _norm_proj_kernel(*refs, n_first):
    if n_first is None:
        x_ref, g_ref, w_ref, pr_ref, qa_ref = refs
        x = x_ref[...]
    else:
        xa_ref, xb_ref, g_ref, w_ref, pr_ref, qa_ref = refs
        x = _pick_rows(xa_ref, xb_ref, n_first)
    h = _rms_rows(x, g_ref[...])
    y = _dot(_bf(h), w_ref[...])
    pr_ref[...] = y[:, :N_SHIFT]
    qa_ref[...] = y[:, N_SHIFT:]


def _norm_proj(xs, g, w_bf16):
    m = sum(a.shape[0] for a in xs)
    d = xs[0].shape[1]
    n_in = w_bf16.shape[1]
    tm = _pick_tile(xs[-1].shape[0] if len(xs) == 2 else m, (512, 256, 128, 64, 32, 16, 8))
    if len(xs) == 2:
        n_first = xs[0].shape[0] // tm
        x_specs = _dual_specs(tm, d, n_first)
    else:
        n_first = None
        x_specs = [pl.BlockSpec((tm, d), lambda i: (i, 0))]
    return pl.pallas_call(
        functools.partial(_norm_proj_kernel, n_first=n_first),
        out_shape=(jax.ShapeDtypeStruct((m, N_SHIFT), F32), jax.ShapeDtypeStruct((m, N_QKV), F32)),
        grid=(m // tm,),
        in_specs=x_specs + [_const_spec((1, d)), _const_spec((d, n_in))],
        out_specs=(pl.BlockSpec((tm, N_SHIFT), lambda i: (i, 0)), pl.BlockSpec((tm, N_QKV), lambda i: (i, 0))),
        compiler_params=_cparams("parallel"),
        name="norm_proj",
    )(*xs, g.reshape(1, d), w_bf16)


def _rwkv_prep_kernel(*refs, has_vfirst, bs, tt):
    if has_vfirst:
        (pr_ref, shift_ref, mu_ref, w0_ref, wwa_ref, a0_ref, g2_ref, kk_ref, ka_ref, rk_ref, ones_ref,
         vf_ref, v0_ref, v1_ref, v2_ref,
         r_out, lw_out, k_out, v_out, an_out, b_out, g_out, bonus_out, carry_ref) = refs
    else:
        (pr_ref, shift_ref, mu_ref, w0_ref, wwa_ref, a0_ref, g2_ref, kk_ref, ka_ref, rk_ref, ones_ref,
         r_out, lw_out, k_out, v_out, an_out, b_out, g_out, bonus_out, carry_ref) = refs
    rows, n = pr_ref.shape
    j = pl.program_id(1)

    x = pr_ref[...]
    first3 = jnp.where(j == 0, shift_ref[...], carry_ref[...])
    carry_ref[...] = x.reshape(bs, tt, n)[:, tt - 1:tt, :]
    first = jnp.broadcast_to(first3, (bs, tt, n)).reshape(rows, n)
    rolled = pltpu.roll(x, 1, 0)
    tpos = lax.broadcasted_iota(jnp.int32, (rows, n), 0) % tt
    prev = jnp.where(tpos == 0, first, rolled)
    prm = x + (prev - x) * mu_ref[...]

    r = prm[:, 0:C_RWKV]
    k = prm[:, C_RWKV:2 * C_RWKV]
    v = prm[:, 2 * C_RWKV:3 * C_RWKV]
    wa = prm[:, 3 * C_RWKV:3 * C_RWKV + LANES]
    gl = prm[:, 3 * C_RWKV + LANES:3 * C_RWKV + 2 * LANES]

    lane = lax.broadcasted_iota(jnp.int32, wa.shape, 1)
    wa_in = _bf(jnp.where(lane < D_DECAY, jnp.tanh(wa), wa))
    wa_out = _dot(wa_in, wwa_ref[...])
    z = w0_ref[...] + wa_out[:, :C_RWKV]
    softplus = jnp.maximum(-z, 0.0) + jnp.log(1.0 + jnp.exp(-jnp.abs(z)))
    lw = -jnp.exp(-softplus - 0.5)
    a = _sigmoid(a0_ref[...] + wa_out[:, C_RWKV:])
    g = _dot(_bf(_sigmoid(gl)), g2_ref[...])

    if has_vfirst:
        t1 = _dot(_bf(v), v1_ref[...])
        t2 = _dot(_bf(t1), v2_ref[...])
        v = v + (vf_ref[...] - v) * _sigmoid(v0_ref[...] + t2)

    kk = k * kk_ref[...]
    norm = jnp.maximum(jnp.sqrt(_head_sum(kk * kk, ones_ref)), 1e-12)
    kk = kk / norm
    k = k * (1.0 + (a - 1.0) * ka_ref[...])
    bonus = _head_sum(r * k * rk_ref[...], ones_ref) * v

    r_out[...] = r
    lw_out[...] = lw
    k_out[...] = k
    v_out[...] = v
    an_out[...] = -kk
    b_out[...] = kk * a
    g_out[...] = g
    bonus_out[...] = bonus


def _rwkv_prep(pr, row0, nseq, t, shift, p, vfirst, ones128):
    n = pr.shape[1]
    if t >= 64:
        bs, tt = 1, _pick_tile(t, (256, 128, 64))
    else:
        bs, tt = _pick_tile(nseq, (16, 8, 4, 2, 1)), t
    rows = bs * tt
    nt = t // tt
    blk0 = row0 // rows
    ins = [pr, shift.reshape(nseq, 1, n), p["mu"], p["w0"], p["wwa"], p["a0"], p["g2"], p["k_k"], p["k_a"],
           p["r_k"], ones128]
    out_spec = pl.BlockSpec((rows, C_RWKV), lambda i, j: (i * nt + j, 0))
    specs = [pl.BlockSpec((rows, n), lambda i, j: (blk0 + i * nt + j, 0)),
             pl.BlockSpec((bs, 1, n), lambda i, j: (i, 0, 0))] + [_const_spec(a.shape) for a in ins[2:]]
    has_vfirst = vfirst is not None
    if has_vfirst:
        extra = [vfirst, p["v0"], p["v1"], p["v2"]]
        ins += extra
        specs += [out_spec] + [_const_spec(a.shape) for a in extra[1:]]
    out_sds = jax.ShapeDtypeStruct((nseq * t, C_RWKV), F32)
    return pl.pallas_call(
        functools.partial(_rwkv_prep_kernel, has_vfirst=has_vfirst, bs=bs, tt=tt),
        out_shape=(out_sds,) * 8,
        grid=(nseq // bs, nt),
        in_specs=specs,
        out_specs=(out_spec,) * 8,
        scratch_shapes=[pltpu.VMEM((bs, 1, n), F32)],
        compiler_params=_cparams("parallel", "arbitrary"),
        name="rwkv_prep",
    )(*ins)


def _scan_chunk(r, lw, k, v, an, b, s_list, chunk):
    C = chunk
    H = range(H_RWKV)
    row = lax.broadcasted_iota(jnp.int32, (C, C), 0)
    col = lax.broadcasted_iota(jnp.int32, (C, C), 1)
    incl = row >= col
    strict = row > col
    tri = _bf(jnp.where(incl, 1.0, 0.0))
    l1 = _bf(lw)
    rem = lw - l1.astype(F32)
    l2 = _bf(rem)
    l3 = _bf(rem - l2.astype(F32))
    cum = _dot(tri, l1) + _dot(tri, l2) + _dot(tri, l3)
    cum_last = cum[C - 1:C, :]
    e_neg = jnp.exp(-cum)
    e_d = jnp.exp(cum_last - cum)
    pc = jnp.exp(cum_last)
    eye = jnp.where(row == col, 1.0, 0.0)
    hs = lambda x, h: x[:, h * HEAD_DIM:(h + 1) * HEAD_DIM]

    lhs = _bf(jnp.concatenate([an * jnp.exp(cum - lw), r * jnp.exp(cum)], axis=0))
    bq, kq = _bf(b * e_neg), _bf(k * e_neg)
    bd, kd = _bf(b * e_d), _bf(k * e_d)
    vb = _bf(v)
    m_b = [_dot_nt(hs(lhs, h), hs(bq, h)) for h in H]
    m_k = [_dot_nt(hs(lhs, h), hs(kq, h)) for h in H]
    a_ab = [jnp.where(strict, m[:C], 0.0) for m in m_b]
    a_rb = [_bf(jnp.where(incl, m[C:], 0.0)) for m in m_b]
    row2 = lax.broadcasted_iota(jnp.int32, (2 * C, C), 0)
    col2 = lax.broadcasted_iota(jnp.int32, (2 * C, C), 1)
    mask2 = jnp.where(row2 < C, row2, row2 - C + 1) > col2
    a_k = [_bf(jnp.where(mask2, m, 0.0)) for m in m_k]
    kv = [_dot(a_k[h], hs(vb, h)) for h in H]
    skv = [_dot_tn(hs(vb, h), hs(kd, h)) for h in H]

    t_inv = [eye + a for a in a_ab]
    n_sq = max(C.bit_length() - 2, 0)
    if n_sq:
        a_b = [_bf(a) for a in a_ab]
        p_pow = [_dot(a_b[h], a_b[h]) for h in H]
        for lev in range(n_sq):
            p_b = [_bf(p) for p in p_pow]
            t_inv = [t_inv[h] + _dot(_bf(t_inv[h]), p_b[h]) for h in H]
            if lev + 1 < n_sq:
                p_pow = [_dot(p_b[h], p_b[h]) for h in H]
    t_b = [_bf(t) for t in t_inv]
    w1 = [_dot(t_b[h], hs(lhs[:C], h)) for h in H]
    u0 = [_dot(t_b[h], _bf(kv[h][:C])) for h in H]
    s_b = [_bf(s) for s in s_list]
    x1 = [_dot_nt(jnp.concatenate([_bf(w1[h]), hs(lhs[C:], h)], axis=0), s_b[h]) for h in H]
    u_b = [_bf(x1[h][:C] + u0[h]) for h in H]
    y = [x1[h][C:] + _dot(a_rb[h], u_b[h]) + kv[h][C:] for h in H]
    s_new = [s_list[h] * hs(pc, h) + _dot_tn(u_b[h], hs(bd, h)) + skv[h] for h in H]
    return jnp.concatenate(y, axis=1), s_new


def _scan_kernel(r_ref, lw_ref, k_ref, v_ref, an_ref, b_ref, s0_ref, y_ref, sout_ref, s_scr, *, chunk, nb):
    c = pl.program_id(1)

    @pl.when(c == 0)
    def _():
        s_scr[...] = s0_ref[...]

    def body(i, carry):
        rows = pl.ds(0, chunk) if nb == 1 else pl.ds(pl.multiple_of(i * chunk, chunk), chunk)
        s_list = [s_scr[i, h] for h in range(H_RWKV)]
        y, s_new = _scan_chunk(r_ref[rows, :], lw_ref[rows, :], k_ref[rows, :], v_ref[rows, :], an_ref[rows, :],
                               b_ref[rows, :], s_list, chunk)
        y_ref[rows, :] = y
        for h in range(H_RWKV):
            s_scr[i, h] = s_new[h]
        return carry

    if nb == 1:
        body(0, 0)
    else:
        lax.fori_loop(0, nb, body, 0)

    @pl.when(c == pl.num_programs(1) - 1)
    def _():
        sout_ref[...] = s_scr[...]


def _rwkv_scan(r, lw, k, v, an, b, s0, t):
    nseq = s0.shape[0]
    chunk = _pick_tile(t, (64, 32, 16, 8))
    nb = 1 if t > chunk else _pick_tile(nseq, (8, 4, 2, 1))
    nc = t // chunk
    seq_spec = pl.BlockSpec((nb * chunk, C_RWKV), lambda i, j: (i * nc + j, 0))
    st_spec = pl.BlockSpec((nb, H_RWKV, HEAD_DIM, HEAD_DIM), lambda i, j: (i, 0, 0, 0))
    return pl.pallas_call(
        functools.partial(_scan_kernel, chunk=chunk, nb=nb),
        out_shape=(jax.ShapeDtypeStruct((nseq * t, C_RWKV), F32),
                   jax.ShapeDtypeStruct((nseq, H_RWKV, HEAD_DIM, HEAD_DIM), F32)),
        grid=(nseq // nb, nc),
        in_specs=[seq_spec] * 6 + [st_spec],
        out_specs=(seq_spec, st_spec),
        scratch_shapes=[pltpu.VMEM((nb, H_RWKV, HEAD_DIM, HEAD_DIM), F32)],
        compiler_params=_cparams("parallel", "arbitrary"),
        name="rwkv_scan",
    )(r, lw, k, v, an, b, s0)


def _qk_norm(x, g, ones_ref):
    ms = _head_sum(x * x, ones_ref) * (1.0 / HEAD_DIM)
    return x * lax.rsqrt(ms + RMS_EPS) * g


def _sink_softmax_rows(s_parts, sink):
    m = sink
    for s in s_parts:
        m = jnp.maximum(m, jnp.max(s, axis=-1, keepdims=True))
    ps = [jnp.exp(s - m) for s in s_parts]
    den = jnp.exp(sink - m)
    for p in ps:
        den = den + jnp.sum(p, axis=-1, keepdims=True)
    inv = 1.0 / den
    return [p * inv for p in ps]


def _group_rows(q, g, t):
    return jnp.concatenate(
        [q[:, (g * GQA_GROUP + a) * HEAD_DIM:(g * GQA_GROUP + a + 1) * HEAD_DIM] for a in range(GQA_GROUP)], axis=0)


def _group_sinks(sink_ref, g, rows, t):
    rowh = lax.broadcasted_iota(jnp.int32, (rows, 1), 0) // t
    sink = jnp.zeros((rows, 1), F32)
    for a in range(GQA_GROUP):
        sink = jnp.where(rowh == a, sink_ref[g * GQA_GROUP + a], sink)
    return sink


def _swa_prompt_kernel(sink_ref, cur_ref, prev_ref, qn_ref, kn_ref, ones_ref, o_ref, kc_ref):
    n = pl.program_id(1)
    L = ATTN_BLOCK
    rows = GQA_GROUP * L
    cur = cur_ref[...]
    prev = prev_ref[...]
    q = _bf(_qk_norm(cur[:, :C_ATTN], qn_ref[...], ones_ref))
    k_cur = _qk_norm(cur[:, C_ATTN:C_ATTN + C_KV], kn_ref[...], ones_ref)
    k_prev = _qk_norm(prev[:, C_ATTN:C_ATTN + C_KV], kn_ref[...], ones_ref)
    kc_ref[0] = k_cur
    kk = _bf(jnp.concatenate([k_prev, k_cur], axis=0))
    vv = _bf(jnp.concatenate([prev[:, C_ATTN + C_KV:], cur[:, C_ATTN + C_KV:]], axis=0))
    qq = lax.broadcasted_iota(jnp.int32, (rows, 2 * L), 0) % L
    jj = lax.broadcasted_iota(jnp.int32, (rows, 2 * L), 1)
    mask = (jj > qq) & (jj <= qq + L) & ((jj >= L) | (n > 0))
    G = range(KV_HEADS)
    ksl = lambda x, g: x[:, g * HEAD_DIM:(g + 1) * HEAD_DIM]
    s = [jnp.where(mask, _dot_nt(_group_rows(q, g, L), ksl(kk, g)) * (HEAD_DIM ** -0.5), NEG) for g in G]
    p = [_sink_softmax_rows([s[g]], _group_sinks(sink_ref, g, rows, L))[0] for g in G]
    o = [_dot(_bf(p[g]), ksl(vv, g)) for g in G]
    o_ref[...] = jnp.concatenate([o[g][a * L:(a + 1) * L, :] for g in G for a in range(GQA_GROUP)], axis=1)


def _swa_prompt(qa, bsz, t, sinks, qn, kn, ones128):
    L = ATTN_BLOCK
    nb = t // L
    return pl.pallas_call(
        _swa_prompt_kernel,
        out_shape=(jax.ShapeDtypeStruct((bsz * t, C_ATTN), F32), jax.ShapeDtypeStruct((bsz, L, C_KV), F32)),
        grid_spec=pltpu.PrefetchScalarGridSpec(
            num_scalar_prefetch=1,
            grid=(bsz, nb),
            in_specs=[pl.BlockSpec((L, N_QKV), lambda b, n, s: (b * nb + n, 0)),
                      pl.BlockSpec((L, N_QKV), lambda b, n, s: (b * nb + jnp.maximum(n - 1, 0), 0)),
                      pl.BlockSpec((1, C_ATTN), lambda b, n, s: (0, 0)),
                      pl.BlockSpec((1, C_KV), lambda b, n, s: (0, 0)),
                      pl.BlockSpec((LANES, LANES), lambda b, n, s: (0, 0))],
            out_specs=(pl.BlockSpec((L, C_ATTN), lambda b, n, s: (b * nb + n, 0)),
                       pl.BlockSpec((1, L, C_KV), lambda b, n, s: (b, 0, 0)))),
        compiler_params=_cparams("parallel", "arbitrary"),
        name="swa_prompt",
    )(sinks, qa, qa, qn, kn, ones128)


def _swa_sample_kernel(sink_ref, qa_ref, ck_ref, cv_ref, qn_ref, kn_ref, ones_ref, o_ref, nk_ref, nv_ref, *, bb, t):
    W = WINDOW
    rows = GQA_GROUP * t
    rr = lax.broadcasted_iota(jnp.int32, (rows, W), 0) % t
    jc = lax.broadcasted_iota(jnp.int32, (rows, W), 1)
    mask_c = jc > rr
    rn = lax.broadcasted_iota(jnp.int32, (rows, t), 0) % t
    jn = lax.broadcasted_iota(jnp.int32, (rows, t), 1)
    mask_n = jn <= rn
    G = range(KV_HEADS)
    ksl = lambda x, g: x[:, g * HEAD_DIM:(g + 1) * HEAD_DIM]
    scale = HEAD_DIM ** -0.5

    def body(i, carry):
        x = qa_ref[pl.ds(pl.multiple_of(i * t, t), t), :]
        q = _bf(_qk_norm(x[:, :C_ATTN], qn_ref[...], ones_ref))
        k_new = _qk_norm(x[:, C_ATTN:C_ATTN + C_KV], kn_ref[...], ones_ref)
        v_new = x[:, C_ATTN + C_KV:]
        ck = ck_ref[i]
        cv = cv_ref[i]
        nk_ref[i, 0:W - t, :] = ck[t:, :]
        nk_ref[i, W - t:W, :] = k_new
        nv_ref[i, 0:W - t, :] = cv[t:, :]
        nv_ref[i, W - t:W, :] = v_new
        ckb, cvb, knb, vnb = _bf(ck), _bf(cv), _bf(k_new), _bf(v_new)
        qg = [_group_rows(q, g, t) for g in G]
        s_c = [jnp.where(mask_c, _dot_nt(qg[g], ksl(ckb, g)) * scale, NEG) for g in G]
        s_n = [jnp.where(mask_n, _dot_nt(qg[g], ksl(knb, g)) * scale, NEG) for g in G]
        p = [_sink_softmax_rows([s_c[g], s_n[g]], _group_sinks(sink_ref, g, rows, t)) for g in G]
        og = [_dot(_bf(p[g][0]), ksl(cvb, g)) + _dot(_bf(p[g][1]), ksl(vnb, g)) for g in G]
        o_ref[pl.ds(pl.multiple_of(i * t, t), t), :] = jnp.concatenate(
            [og[g][a * t:(a + 1) * t, :] for g in G for a in range(GQA_GROUP)], axis=1)
        return carry

    lax.fori_loop(0, bb, body, 0)


def _swa_sample(qa, row0, bsz, t, ck, cv, sinks, qn, kn, ones128):
    W = WINDOW
    bb = _pick_tile(bsz, (16, 8, 4, 2, 1))
    blk0 = row0 // (bb * t)
    sds = jax.ShapeDtypeStruct
    return pl.pallas_call(
        functools.partial(_swa_sample_kernel, bb=bb, t=t),
        out_shape=(sds((bsz * t, C_ATTN), F32), sds((bsz, W, C_KV), F32), sds((bsz, W, C_KV), F32)),
        grid_spec=pltpu.PrefetchScalarGridSpec(
            num_scalar_prefetch=1,
            grid=(bsz // bb,),
            in_specs=[pl.BlockSpec((bb * t, N_QKV), lambda b, s: (blk0 + b, 0)),
                      pl.BlockSpec((bb, W, C_KV), lambda b, s: (b, 0, 0)),
                      pl.BlockSpec((bb, W, C_KV), lambda b, s: (b, 0, 0)),
                      pl.BlockSpec((1, C_ATTN), lambda b, s: (0, 0)),
                      pl.BlockSpec((1, C_KV), lambda b, s: (0, 0)),
                      pl.BlockSpec((LANES, LANES), lambda b, s: (0, 0))],
            out_specs=(pl.BlockSpec((bb * t, C_ATTN), lambda b, s: (b, 0)),
                       pl.BlockSpec((bb, W, C_KV), lambda b, s: (b, 0, 0)),
                       pl.BlockSpec((bb, W, C_KV), lambda b, s: (b, 0, 0)))),
        compiler_params=_cparams("parallel"),
        name="swa_sample",
    )(sinks, qa, ck, cv, qn, kn, ones128)


def _post_kernel(*refs, has_router, n_first, n_x):
    xs, refs = refs[:n_x], refs[n_x:]
    pairs, refs = refs[:8], refs[8:]
    if has_router:
        (lnw_ref, lnb_ref, wt_ref, wb_ref, gf_ref, ones_ref, wrh_ref, wrl_ref, br_ref,
         xmid_ref, h_ref, route_ref) = refs
    else:
        lnw_ref, lnb_ref, wt_ref, wb_ref, gf_ref, ones_ref, xmid_ref, h_ref = refs
    x = xs[0][...] if n_x == 1 else _pick_rows(xs[0], xs[1], n_first)
    y, bonus, g, o_a = [_pick_rows(pairs[2 * i], pairs[2 * i + 1], n_first) for i in range(4)]
    mu = _head_sum(y, ones_ref) * (1.0 / HEAD_DIM)
    d = y - mu
    var = _head_sum(d * d, ones_ref) * (1.0 / HEAD_DIM)
    yn = d * lax.rsqrt(var + LNX_EPS) * lnw_ref[...] + lnb_ref[...]
    o_r = (yn + bonus) * g
    mixed = _dot(_bf(o_r), wt_ref[...]) + _dot(_bf(o_a), wb_ref[...])
    x_mid = x + mixed
    xmid_ref[...] = x_mid
    h = _rms_rows(x_mid, gf_ref[...])
    h_ref[...] = h.astype(h_ref.dtype)
    if has_router:
        hh, hl = _split2(h)
        logits = _dot(hh, wrh_ref[...]) + _dot(hh, wrl_ref[...]) + _dot(hl, wrh_ref[...]) + br_ref[...]
        lane = lax.broadcasted_iota(jnp.int32, logits.shape, 1)
        lg = jnp.where(lane < N_EXPERTS, logits, -jnp.inf)
        m1 = jnp.max(lg, axis=-1, keepdims=True)
        i1 = jnp.min(jnp.where(lg == m1, lane, LANES), axis=-1, keepdims=True)
        lg2 = jnp.where(lane == i1, -jnp.inf, lg)
        m2 = jnp.max(lg2, axis=-1, keepdims=True)
        i2 = jnp.min(jnp.where(lg2 == m2, lane, LANES), axis=-1, keepdims=True)
        e = jnp.exp(m2 - m1)
        g1 = 1.0 / (1.0 + e)
        g2 = e * g1
        route_ref[...] = jnp.where(lane == 0, i1.astype(F32),
                                   jnp.where(lane == 1, i2.astype(F32),
                                             jnp.where(lane == 2, g1, jnp.where(lane == 3, g2, 0.0))))


def _post(xs, pairs, lnw, lnb, w_top, w_bot, g_ffn, ones128, router, h_dtype):
    d = xs[0].shape[1]
    m = sum(a.shape[0] for a in pairs[0])
    tm = _pick_tile(pairs[0][1].shape[0], (256, 128, 64, 32, 16, 8))
    n_first = pairs[0][0].shape[0] // tm
    row = lambda w: pl.BlockSpec((tm, w), lambda i: (i, 0))
    consts = [lnw, lnb, w_top, w_bot, g_ffn, ones128] + (list(router) if router is not None else [])
    ins = list(xs) + [a for pr_ in pairs for a in pr_] + consts
    specs = (_dual_specs(tm, d, n_first) if len(xs) == 2 else [row(d)])
    for _ in pairs:
        specs += _dual_specs(tm, C_RWKV, n_first)
    specs += [_const_spec(a.shape) for a in consts]
    outs = [jax.ShapeDtypeStruct((m, d), F32), jax.ShapeDtypeStruct((m, d), h_dtype)]
    out_specs = [row(d), row(d)]
    if router is not None:
        outs.append(jax.ShapeDtypeStruct((m, LANES), F32))
        out_specs.append(row(LANES))
    return pl.pallas_call(
        functools.partial(_post_kernel, has_router=router is not None, n_first=n_first, n_x=len(xs)),
        out_shape=tuple(outs),
        grid=(m // tm,),
        in_specs=specs,
        out_specs=tuple(out_specs),
        compiler_params=_cparams("parallel"),
        name="post_mix",
    )(*ins)


def _ffn_kernel(*refs, has_res):
    if has_res:
        be_ref, x_ref, wg_ref, wu_ref, wd_ref, res_ref, o_ref = refs
    else:
        be_ref, x_ref, wg_ref, wu_ref, wd_ref, o_ref = refs
    del be_ref
    x = _bf(x_ref[...])
    d_ff = wg_ref.shape[2]
    acc = None
    for f in range(d_ff // FF_CHUNK):
        fs = slice(f * FF_CHUNK, (f + 1) * FF_CHUNK)
        gt = _dot(x, wg_ref[0, :, fs])
        up = _dot(x, wu_ref[0, :, fs])
        act = _bf(gt * _sigmoid(gt) * up)
        part = _dot(act, wd_ref[0, fs, :])
        acc = part if acc is None else acc + part
    if has_res:
        acc = acc + res_ref[...]
    o_ref[...] = acc


def _ffn(x, block_e, wg, wu, wd, res, tm):
    m, d = x.shape
    d_ff = wg.shape[2]
    row = pl.BlockSpec((tm, d), lambda i, be: (i, 0))
    ins = [x, wg, wu, wd]
    specs = [row,
             pl.BlockSpec((1, d, d_ff), lambda i, be: (be[i], 0, 0)),
             pl.BlockSpec((1, d, d_ff), lambda i, be: (be[i], 0, 0)),
             pl.BlockSpec((1, d_ff, d), lambda i, be: (be[i], 0, 0))]
    if res is not None:
        ins.append(res)
        specs.append(row)
    return pl.pallas_call(
        functools.partial(_ffn_kernel, has_res=res is not None),
        out_shape=jax.ShapeDtypeStruct((m, d), F32),
        grid_spec=pltpu.PrefetchScalarGridSpec(
            num_scalar_prefetch=1, grid=(m // tm,), in_specs=specs, out_specs=row),
        compiler_params=_cparams("arbitrary"),
        name="swiglu",
    )(block_e, *ins)


def _gather_kernel(idx_ref, src_ref, o_ref, sem, *, rows):
    base = pl.program_id(0) * rows

    def start(g, carry):
        for u in range(DMA_UNROLL):
            r = g * DMA_UNROLL + u
            pltpu.make_async_copy(src_ref.at[pl.ds(idx_ref[base + r], 1)], o_ref.at[pl.ds(r, 1)],
                                  sem).start(priority=u % 2)
        return carry

    lax.fori_loop(0, rows // DMA_UNROLL, start, 0)
    pltpu.make_async_copy(src_ref.at[pl.ds(0, rows)], o_ref, sem).wait()


def _gather_rows(src, idx, rows):
    n = idx.shape[0]
    d = src.shape[1]
    return pl.pallas_call(
        functools.partial(_gather_kernel, rows=rows),
        out_shape=jax.ShapeDtypeStruct((n, d), src.dtype),
        grid_spec=pltpu.PrefetchScalarGridSpec(
            num_scalar_prefetch=1, grid=(n // rows,),
            in_specs=[pl.BlockSpec(memory_space=pl.ANY)],
            out_specs=pl.BlockSpec((rows, d), lambda i, idx: (i, 0)),
            scratch_shapes=[pltpu.SemaphoreType.DMA(())]),
        compiler_params=_cparams("arbitrary"),
        name="gather_rows",
    )(idx, src)


def _combine_kernel(s0_ref, s1_ref, x_ref, route_ref, yb_ref, o_ref, buf, sem, *, rows):
    base = pl.program_id(0) * rows

    def start(g, carry):
        for u in range(DMA_UNROLL // 2):
            r = g * (DMA_UNROLL // 2) + u
            pltpu.make_async_copy(yb_ref.at[pl.ds(s0_ref[base + r], 1)], buf.at[0, pl.ds(r, 1)],
                                  sem.at[0]).start(priority=0)
            pltpu.make_async_copy(yb_ref.at[pl.ds(s1_ref[base + r], 1)], buf.at[1, pl.ds(r, 1)],
                                  sem.at[1]).start(priority=1)
        return carry

    lax.fori_loop(0, rows // (DMA_UNROLL // 2), start, 0)
    pltpu.make_async_copy(yb_ref.at[pl.ds(0, rows)], buf.at[0], sem.at[0]).wait()
    pltpu.make_async_copy(yb_ref.at[pl.ds(0, rows)], buf.at[1], sem.at[1]).wait()
    route = route_ref[...]
    o_ref[...] = x_ref[...] + route[:, 2:3] * buf[0] + route[:, 3:4] * buf[1]


def _moe_combine(x_mid, route, yb, slot0, slot1):
    m, d = x_mid.shape
    rows = _pick_tile(m, (512, 256, 128, 64, 32, 16, 8))
    row = lambda w: pl.BlockSpec((rows, w), lambda i, a, b: (i, 0))
    return pl.pallas_call(
        functools.partial(_combine_kernel, rows=rows),
        out_shape=jax.ShapeDtypeStruct((m, d), F32),
        grid_spec=pltpu.PrefetchScalarGridSpec(
            num_scalar_prefetch=2, grid=(m // rows,),
            in_specs=[row(d), row(LANES), pl.BlockSpec(memory_space=pl.ANY)],
            out_specs=row(d),
            scratch_shapes=[pltpu.VMEM((2, rows, d), F32), pltpu.SemaphoreType.DMA((2,))]),
        compiler_params=_cparams("arbitrary"),
        name="moe_combine",
    )(slot0, slot1, x_mid, route, yb)


def _moe_plan(route, tm):
    m = route.shape[0]
    a_tot = m * TOP_K
    e_flat = route[:, :TOP_K].astype(jnp.int32).reshape(a_tot)
    tok_flat = jnp.arange(a_tot, dtype=jnp.int32) // TOP_K
    onehot = (e_flat[:, None] == jnp.arange(N_EXPERTS, dtype=jnp.int32)[None, :]).astype(jnp.int32)
    csum = jnp.cumsum(onehot, axis=0)
    rank = jnp.sum((csum - onehot) * onehot, axis=1)
    counts = csum[-1]
    padded = (counts + tm - 1) // tm * tm
    pad_end = jnp.cumsum(padded)
    pad_start = pad_end - padded
    slot = (jnp.sum(pad_start[None, :] * onehot, axis=1) + rank).astype(jnp.int32)
    n_blocks = -(-(a_tot + N_EXPERTS * (tm - 1)) // tm)
    slot_tok = jnp.zeros((n_blocks * tm,), jnp.int32).at[slot].set(tok_flat)
    blk_start = (jnp.arange(n_blocks, dtype=jnp.int32) * tm)[:, None]
    block_e = jnp.minimum(jnp.sum((pad_end[None, :] <= blk_start).astype(jnp.int32), axis=1), N_EXPERTS - 1)
    slot_of = slot.reshape(m, TOP_K)
    return slot_tok, block_e.astype(jnp.int32), slot_of[:, 0], slot_of[:, 1]


def _ones_block():
    i = jnp.arange(LANES) // HEAD_DIM
    return (i[:, None] == i[None, :]).astype(BF16)


def kernel(x_prompt, x_sample, state_wkv, state_shift, cache_swa_k, cache_swa_v, norm_mix, norm_ffn, w_in, mu_shift, w0, w2, a0, a2, g2, k_k, k_a, r_k, ln_x_w, ln_x_b, v0, v1, v2, q_norm, k_norm, sinks, w_out, w_gate_d, w_up_d, w_down_d, w_router, b_router, w_gate_e, w_up_e, w_down_e):
    depth = w_in.shape[0]
    bp, tp, d = x_prompt.shape
    bs, ts, _ = x_sample.shape
    mp, ms = bp * tp, bs * ts
    m = mp + ms
    ones128 = _ones_block()
    xs = (x_prompt.reshape(mp, d), x_sample.reshape(ms, d))
    row = lambda a: a.reshape(1, -1).astype(F32)

    shift_zero = jnp.zeros((bp, N_SHIFT), F32)
    wkv_zero = jnp.zeros((bp, H_RWKV, HEAD_DIM, HEAD_DIM), F32)
    vfirst_p = vfirst_s = None
    outs = {k_: [] for k_ in ("wkv_p", "wkv_s", "sh_p", "sh_s", "k_p", "v_p", "k_s", "v_s")}

    for l in range(depth):
        pr, qa = _norm_proj(xs, norm_mix[l], w_in[l].astype(BF16))
        outs["sh_p"].append(pr[tp - 1:mp:tp])
        outs["sh_s"].append(pr[mp + ts - 1::ts])

        wwa = jnp.zeros((LANES, 2 * C_RWKV), F32)
        wwa = wwa.at[:D_DECAY, :C_RWKV].set(w2[l]).at[D_DECAY:, C_RWKV:].set(a2[l]).astype(BF16)
        prm = dict(mu=row(mu_shift[l]), w0=row(w0[l]), wwa=wwa, a0=row(a0[l]), g2=g2[l].astype(BF16),
                   k_k=row(k_k[l]), k_a=row(k_a[l]), r_k=row(r_k[l]))
        if l > 0:
            d_mv = v1.shape[2]
            prm["v0"] = row(v0[l - 1])
            prm["v1"] = jnp.zeros((C_RWKV, LANES), F32).at[:, :d_mv].set(v1[l - 1]).astype(BF16)
            prm["v2"] = jnp.zeros((LANES, C_RWKV), F32).at[:d_mv].set(v2[l - 1]).astype(BF16)
        r_p, lw_p, k_p, v_p, an_p, b_p, g_p, bonus_p = _rwkv_prep(pr, 0, bp, tp, shift_zero, prm, vfirst_p, ones128)
        r_s, lw_s, k_s, v_s, an_s, b_s, g_s, bonus_s = _rwkv_prep(pr, mp, bs, ts, state_shift[l], prm, vfirst_s,
                                                                 ones128)
        if l == 0:
            vfirst_p, vfirst_s = v_p, v_s

        y_p, wkv_p = _rwkv_scan(r_p, lw_p, k_p, v_p, an_p, b_p, wkv_zero, tp)
        y_s, wkv_s = _rwkv_scan(r_s, lw_s, k_s, v_s, an_s, b_s, state_wkv[l], ts)
        outs["wkv_p"].append(wkv_p)
        outs["wkv_s"].append(wkv_s)

        qn = jnp.tile(row(q_norm[l]), (1, H_ATTN))
        kn = jnp.tile(row(k_norm[l]), (1, KV_HEADS))
        sk = sinks[l].astype(F32)
        oa_p, kc_p = _swa_prompt(qa, bp, tp, sk, qn, kn, ones128)
        ck = cache_swa_k[l].reshape(bs, WINDOW, C_KV)
        cv = cache_swa_v[l].reshape(bs, WINDOW, C_KV)
        oa_s, nk_s, nv_s = _swa_sample(qa, mp, bs, ts, ck, cv, sk, qn, kn, ones128)
        outs["k_p"].append(kc_p.reshape(bp, WINDOW, KV_HEADS, HEAD_DIM))
        v_tail = qa[:mp].reshape(bp, tp, N_QKV)[:, tp - WINDOW:, C_ATTN + C_KV:]
        outs["v_p"].append(v_tail.reshape(bp, WINDOW, KV_HEADS, HEAD_DIM))
        outs["k_s"].append(nk_s.reshape(bs, WINDOW, KV_HEADS, HEAD_DIM))
        outs["v_s"].append(nv_s.reshape(bs, WINDOW, KV_HEADS, HEAD_DIM))

        wo = w_out[l].astype(BF16)
        moe = l % 2 == 1
        j = l // 2
        router = None
        if moe:
            wr = jnp.zeros((d, LANES), F32).at[:, :N_EXPERTS].set(w_router[j])
            wr_hi = wr.astype(BF16)
            wr_lo = (wr - wr_hi.astype(F32)).astype(BF16)
            br = jnp.zeros((1, LANES), F32).at[0, :N_EXPERTS].set(b_router[j])
            router = (wr_hi, wr_lo, br)
        post = _post(xs, [(y_p, y_s), (bonus_p, bonus_s), (g_p, g_s), (oa_p, oa_s)],
                     row(ln_x_w[l]), row(ln_x_b[l]), wo[:C_RWKV], wo[C_RWKV:], row(norm_ffn[l]), ones128,
                     router, F32 if moe else BF16)
        if not moe:
            x_mid, h = post
            tm = _pick_tile(m, (512, 256, 128, 64, 32, 16, 8))
            x = _ffn(h, jnp.zeros((m // tm,), jnp.int32), w_gate_d[j][None].astype(BF16),
                     w_up_d[j][None].astype(BF16), w_down_d[j][None].astype(BF16), x_mid, tm)
        else:
            x_mid, h, route = post
            tm = 512 if m >= 4096 else 16
            slot_tok, block_e, slot0, slot1 = _moe_plan(route, tm)
            xg = _gather_rows(h, slot_tok, _pick_tile(slot_tok.shape[0], (2 * tm, tm)))
            yb = _ffn(xg, block_e, w_gate_e[j].astype(BF16), w_up_e[j].astype(BF16), w_down_e[j].astype(BF16),
                      None, tm)
            x = _moe_combine(x_mid, route, yb, slot0, slot1)
        xs = (x,)

    st = lambda name: jnp.stack(outs[name])
    return (x[:mp].reshape(bp, tp, d), x[mp:].reshape(bs, ts, d), st("wkv_p"), st("wkv_s"), st("sh_p"), st("sh_s"),
            st("k_p"), st("v_p"), st("k_s"), st("v_s"))
```
